```python
import math
import jax
import jax.numpy as jnp
from jax import lax
import numpy as np

D_MODEL = 1024
BATCH = 2
SEQ = 8192
DEPTH = 2
DEC_BATCH = 128
DEC_SEQ = 4
PAST_LEN = 2048
PAGE_SIZE = 128

N_EVEN = (DEPTH + 1) // 2
N_ODD = DEPTH // 2

SSD_D_INNER = D_MODEL
SSD_HEADDIM = 64
SSD_HEADS = SSD_D_INNER // SSD_HEADDIM
SSD_GROUPS = 2
SSD_STATE = 128
SSD_CONV = 4
SSD_CHUNK = 128
SSD_CONV_DIM = SSD_D_INNER + 2 * SSD_GROUPS * SSD_STATE

NSA_HEADS = 16
NSA_HEADDIM = 64
NSA_KV_GROUPS = 2
NSA_REP = NSA_HEADS // NSA_KV_GROUPS
NSA_Q_DIM = NSA_HEADS * NSA_HEADDIM
NSA_KV_DIM = NSA_KV_GROUPS * NSA_HEADDIM
CMP_BLOCK = 32
SEL_BLOCK = 64
SEL_TOPK = 16
WINDOW = 512
Q_BLOCK = 128

IN0_SIZES = (SSD_D_INNER, SSD_CONV_DIM, SSD_HEADS, NSA_Q_DIM) + (NSA_KV_DIM,) * 6 + (3 * NSA_HEADS,)
IN0_DIM = sum(IN0_SIZES)
MIX0_DIM = SSD_D_INNER + NSA_Q_DIM

REL_BUCKETS = 32
REL_MAX_DIST = 1024

CONF_CH = D_MODEL
CONF_KERNEL = 31

MEM_TOKENS = 256
MEM_HEADS = 4
MEM_HEADDIM = 128

D_FF = 3584
N_EXPERTS = 8
TOP_K = 2

RMS_EPS = 1e-6
LN_EPS = 1e-5
NEG_INF = -1e30
FORCE_SCORE = 1e30

kernel_name = 'hybrid_ssd_nsa_conformer_decode_step'


def rmsnorm(x, g):
    xf = x.astype(jnp.float32)
    y = xf * lax.rsqrt(jnp.mean(xf * xf, axis=-1, keepdims=True) + RMS_EPS)
    return (y * g.astype(jnp.float32)).astype(x.dtype)


def layernorm(x, g, b):
    xf = x.astype(jnp.float32)
    mu = jnp.mean(xf, axis=-1, keepdims=True)
    var = jnp.mean(jnp.square(xf - mu), axis=-1, keepdims=True)
    y = (xf - mu) * lax.rsqrt(var + LN_EPS) * g.astype(jnp.float32) + b.astype(jnp.float32)
    return y.astype(x.dtype)


def split_last(x, sizes):
    return jnp.split(x, np.cumsum(np.array(sizes))[:-1].tolist(), axis=-1)


def causal_dwconv(x_pad, w):
    return lax.conv_general_dilated(x_pad, w[:, None, :], (1,), 'VALID',
                                    dimension_numbers=('NWC', 'WIO', 'NWC'),
                                    feature_group_count=x_pad.shape[-1])


def rel_bucket(dist):
    n = jnp.maximum(dist, 0)
    max_exact = REL_BUCKETS // 2
    nf = jnp.maximum(n, max_exact).astype(jnp.float32)
    large = max_exact + (jnp.log(nf / max_exact) / math.log(REL_MAX_DIST / max_exact)
                         * (REL_BUCKETS - max_exact)).astype(jnp.int32)
    return jnp.where(n < max_exact, n, jnp.minimum(large, REL_BUCKETS - 1))


def segsum(a):
    T = a.shape[-1]
    ix = jnp.arange(T)
    rep = jnp.broadcast_to(a[..., :, None], a.shape + (T,))
    cs = jnp.cumsum(jnp.where(ix[:, None] > ix[None, :], rep, 0.0), axis=-2)
    return jnp.where(ix[:, None] >= ix[None, :], cs, -jnp.inf)


def ssd_scan(x, dt, a, b_in, c_in, h0):
    bt, L, H, P = x.shape
    G, N = b_in.shape[-2], b_in.shape[-1]
    R = H // G
    Q = SSD_CHUNK if L % SSD_CHUNK == 0 else L
    nc = L // Q
    xd = (x * dt[..., None]).reshape(bt, nc, Q, G, R, P)
    ad = (dt * a).reshape(bt, nc, Q, G, R).transpose(0, 3, 4, 1, 2)
    bc = b_in.reshape(bt, nc, Q, G, N)
    cc = c_in.reshape(bt, nc, Q, G, N)
    a_cum = jnp.cumsum(ad, axis=-1)
    decay_in = jnp.exp(segsum(ad))
    cb = jnp.einsum('bclgn,bcsgn->bgcls', cc, bc)
    y_diag = jnp.einsum('bgcls,bgrcls,bcsgrp->bclgrp', cb, decay_in, xd)
    decay_states = jnp.exp(a_cum[..., -1:] - a_cum)
    states = jnp.einsum('bclgn,bgrcl,bclgrp->bcgrpn', bc, decay_states, xd)
    states = jnp.concatenate([h0.reshape(bt, 1, G, R, P, N), states], axis=1)
    a_last = jnp.pad(a_cum[..., -1], ((0, 0), (0, 0), (0, 0), (1, 0)))
    chunk_decay = jnp.exp(segsum(a_last))
    new_states = jnp.einsum('bgrzc,bcgrpn->bzgrpn', chunk_decay, states)
    y_off = jnp.einsum('bclgn,bcgrpn,bgrcl->bclgrp', cc, new_states[:, :-1], jnp.exp(a_cum))
    y = (y_diag + y_off).reshape(bt, L, H, P)
    return y, new_states[:, -1].reshape(bt, H, P, N)


def pad_rows(k, mult):
    L = k.shape[1]
    lp = -(-L // mult) * mult
    return jnp.pad(k, ((0, 0), (0, lp - L), (0, 0), (0, 0)))


def block_mean(k):
    bt, lk, g, d = k.shape
    return jnp.mean(k.reshape(bt, lk // CMP_BLOCK, CMP_BLOCK, g, d), axis=2)


def sel_blocks(k):
    bt, lk, g, d = k.shape
    return k.reshape(bt, lk // SEL_BLOCK, SEL_BLOCK, g, d).transpose(0, 3, 1, 2, 4)


def nsa_cmp_sel(q, q_pos, kc, vc, ks_blk, vs_blk, rel_g):
    bt, tq = q.shape[0], q.shape[1]
    ncb = kc.shape[1]
    nsb = ks_blk.shape[2]
    blk_end = jnp.arange(ncb) * CMP_BLOCK + (CMP_BLOCK - 1)
    dist_c = q_pos[:, None] - blk_end[None, :]
    valid_c = dist_c >= 0
    bias_c = rel_g[:, rel_bucket(dist_c)].transpose(0, 3, 1, 2)
    s_c = jnp.einsum('btgrd,bcgd->bgrtc', q, kc).astype(jnp.float32) + bias_c
    p_c = jax.nn.softmax(jnp.where(valid_c, s_c, NEG_INF), axis=-1) * valid_c
    o_c = jnp.einsum('bgrtc,bcgd->btgrd', p_c.astype(vc.dtype), vc)
    imp = p_c.sum(axis=2).reshape(bt, NSA_KV_GROUPS, tq, nsb, SEL_BLOCK // CMP_BLOCK).sum(axis=-1)
    cur = (q_pos // SEL_BLOCK)[:, None]
    j = jnp.arange(nsb)[None, :]
    forced = (j == 0) | (j == cur) | (j == cur - 1)
    score = jnp.where(forced, FORCE_SCORE, jnp.where(j <= cur, imp, NEG_INF))
    _, idx = lax.top_k(score, min(SEL_TOPK, nsb))
    b_ix = jnp.arange(bt)[:, None, None, None]
    g_ix = jnp.arange(NSA_KV_GROUPS)[None, :, None, None]
    ks = ks_blk[b_ix, g_ix, idx]
    vs = vs_blk[b_ix, g_ix, idx]
    pos_s = idx[..., None] * SEL_BLOCK + jnp.arange(SEL_BLOCK)
    dist_s = q_pos[None, None, :, None, None] - pos_s
    bias_s = rel_g[g_ix[..., None], rel_bucket(dist_s)].transpose(0, 1, 5, 2, 3, 4)
    s_s = jnp.einsum('btgrd,bgtkjd->bgrtkj', q, ks).astype(jnp.float32) + bias_s
    s_s = jnp.where((dist_s >= 0)[:, :, None], s_s, NEG_INF)
    p_s = jax.nn.softmax(s_s, axis=(-2, -1))
    o_s = jnp.einsum('bgrtkj,bgtkjd->btgrd', p_s.astype(vs.dtype), vs)
    return o_c, o_s


def nsa_window(q, q_pos, kw, vw, k_pos, rel_g):
    dist = q_pos[:, None] - k_pos[None, :]
    valid = (dist >= 0) & (dist < WINDOW) & (k_pos[None, :] >= 0)
    bias = rel_g[:, rel_bucket(dist)].transpose(0, 3, 1, 2)
    s = jnp.einsum('btgrd,bsgd->bgrts', q, kw).astype(jnp.float32) + bias
    p = jax.nn.softmax(jnp.where(valid, s, NEG_INF), axis=-1)
    return jnp.einsum('bgrts,bsgd->btgrd', p.astype(vw.dtype), vw)


def nsa_combine(o_c, o_s, o_w, g):
    bt, t = o_c.shape[0], o_c.shape[1]
    g = g.reshape(bt, t, 3, NSA_KV_GROUPS, NSA_REP, 1)
    o = g[:, :, 0] * o_c + g[:, :, 1] * o_s + g[:, :, 2] * o_w
    return o.reshape(bt, t, NSA_Q_DIM).astype(o_c.dtype)


def nsa_prompt(q, kc_n, vc_n, ks_n, vs_n, kw_n, vw_n, gates, rel_g):
    bt, s = q.shape[0], q.shape[1]
    kc = block_mean(pad_rows(kc_n, SEL_BLOCK))
    vc = block_mean(pad_rows(vc_n, SEL_BLOCK))
    ks = sel_blocks(pad_rows(ks_n, SEL_BLOCK))
    vs = sel_blocks(pad_rows(vs_n, SEL_BLOCK))
    zpad = jnp.zeros((bt, WINDOW) + kw_n.shape[2:], kw_n.dtype)
    kw_p = jnp.concatenate([zpad, kw_n], axis=1)
    vw_p = jnp.concatenate([zpad, vw_n], axis=1)

    def one_block(t0):
        q_pos = t0 + jnp.arange(Q_BLOCK, dtype=jnp.int32)
        qb = lax.dynamic_slice_in_dim(q, t0, Q_BLOCK, axis=1)
        gb = lax.dynamic_slice_in_dim(gates, t0, Q_BLOCK, axis=1)
        o_c, o_s = nsa_cmp_sel(qb, q_pos, kc, vc, ks, vs, rel_g)
        kwb = lax.dynamic_slice_in_dim(kw_p, t0, WINDOW + Q_BLOCK, axis=1)
        vwb = lax.dynamic_slice_in_dim(vw_p, t0, WINDOW + Q_BLOCK, axis=1)
        k_pos = t0 - WINDOW + jnp.arange(WINDOW + Q_BLOCK, dtype=jnp.int32)
        o_w = nsa_window(qb, q_pos, kwb, vwb, k_pos, rel_g)
        return nsa_combine(o_c, o_s, o_w, gb)

    o = lax.map(one_block, jnp.arange(s // Q_BLOCK, dtype=jnp.int32) * Q_BLOCK)
    return o.transpose(1, 0, 2, 3).reshape(bt, s, NSA_Q_DIM)


def nsa_sample(q, kc_n, vc_n, ks_n, vs_n, kw_n, vw_n, gates, wk_buf, wv_buf, page_table, pages, li, rel_g):
    t = q.shape[1]
    past_len = page_table.shape[1] * PAGE_SIZE
    wb = wk_buf.shape[1]
    q_pos = past_len + jnp.arange(t, dtype=jnp.int32)
    k_pos_w = past_len - wb + jnp.arange(wb + t, dtype=jnp.int32)

    def full_rows(cache, pt, new):
        past = cache[pt, li].reshape(past_len, NSA_KV_GROUPS, NSA_HEADDIM)
        return pad_rows(jnp.concatenate([past, new], axis=0)[None], SEL_BLOCK)

    def one_seq(args):
        pt, qi, ckn, cvn, skn, svn, kwn, vwn, wkb, wvb, gi = args
        kc = block_mean(full_rows(pages[0], pt, ckn))
        vc = block_mean(full_rows(pages[1], pt, cvn))
        ks = sel_blocks(full_rows(pages[2], pt, skn))
        vs = sel_blocks(full_rows(pages[3], pt, svn))
        o_c, o_s = nsa_cmp_sel(qi[None], q_pos, kc, vc, ks, vs, rel_g)
        kw = jnp.concatenate([wkb, kwn], axis=0)[None]
        vw = jnp.concatenate([wvb, vwn], axis=0)[None]
        o_w = nsa_window(qi[None], q_pos, kw, vw, k_pos_w, rel_g)
        return nsa_combine(o_c, o_s, o_w, gi[None])[0]

    return lax.map(one_seq, (page_table, q, kc_n, vc_n, ks_n, vs_n, kw_n, vw_n, wk_buf, wv_buf, gates))


def even_mixer(h, li, p, st):
    bt, L, _ = h.shape
    dty = h.dtype
    (z, xbc, dt_raw, q, kc_n, vc_n, ks_n, vs_n, kw_n, vw_n, g_raw) = split_last(h @ p['w_in0'][li], IN0_SIZES)
    if st is None:
        conv_buf = jnp.zeros((bt, SSD_CONV - 1, SSD_CONV_DIM), dty)
        h0 = jnp.zeros((bt, SSD_HEADS, SSD_HEADDIM, SSD_STATE), jnp.float32)
    else:
        conv_buf = st['ssd_conv'][:, li]
        h0 = st['ssm'][:, li].astype(jnp.float32)
    xbc_pad = jnp.concatenate([conv_buf, xbc], axis=1)
    new_conv = xbc_pad[:, -(SSD_CONV - 1):]
    xbc_c = jax.nn.silu(causal_dwconv(xbc_pad, p['ssd_conv_w'][li]) + p['ssd_conv_b'][li]).astype(jnp.float32)
    xs, bs, cs = split_last(xbc_c, (SSD_D_INNER, SSD_GROUPS * SSD_STATE, SSD_GROUPS * SSD_STATE))
    xs = xs.reshape(bt, L, SSD_HEADS, SSD_HEADDIM)
    dt = jax.nn.softplus(dt_raw.astype(jnp.float32) + p['ssd_dt_bias'][li].astype(jnp.float32))
    a = -jnp.exp(p['ssd_a_log'][li].astype(jnp.float32))
    y, h_new = ssd_scan(xs, dt, a, bs.reshape(bt, L, SSD_GROUPS, SSD_STATE),
                        cs.reshape(bt, L, SSD_GROUPS, SSD_STATE), h0)
    y = y + xs * p['ssd_d'][li].astype(jnp.float32)[:, None]
    y = y.reshape(bt, L, SSD_D_INNER) * jax.nn.silu(z.astype(jnp.float32))
    yg = y.reshape(bt, L, SSD_GROUPS, SSD_D_INNER // SSD_GROUPS)
    yg = yg * lax.rsqrt(jnp.mean(yg * yg, axis=-1, keepdims=True) + RMS_EPS)
    y_ssd = (yg.reshape(bt, L, SSD_D_INNER) * p['ssd_norm'][li].astype(jnp.float32)).astype(dty)
    gates = jax.nn.sigmoid(g_raw.astype(jnp.float32)).reshape(bt, L, 3, NSA_HEADS)
    qh = q.reshape(bt, L, NSA_KV_GROUPS, NSA_REP, NSA_HEADDIM) * (NSA_HEADDIM ** -0.5)
    kc_n, vc_n, ks_n, vs_n, kw_n, vw_n = [t.reshape(bt, L, NSA_KV_GROUPS, NSA_HEADDIM)
                                          for t in (kc_n, vc_n, ks_n, vs_n, kw_n, vw_n)]
    rel_g = p['rel_bias'].reshape(REL_BUCKETS, NSA_KV_GROUPS, NSA_REP).transpose(1, 0, 2).astype(jnp.float32)
    if st is None:
        o_nsa = nsa_prompt(qh, kc_n, vc_n, ks_n, vs_n, kw_n, vw_n, gates, rel_g)
        wb = min(WINDOW, L)
        new_wk, new_wv = kw_n[:, -wb:], vw_n[:, -wb:]
    else:
        wk_buf, wv_buf = st['win_k'][:, li], st['win_v'][:, li]
        o_nsa = nsa_sample(qh, kc_n, vc_n, ks_n, vs_n, kw_n, vw_n, gates, wk_buf, wv_buf,
                           st['page_table'], st['pages'], li, rel_g)
        wb = wk_buf.shape[1]
        new_wk = jnp.concatenate([wk_buf, kw_n], axis=1)[:, -wb:]
        new_wv = jnp.concatenate([wv_buf, vw_n], axis=1)[:, -wb:]
    out = jnp.concatenate([y_ssd, o_nsa], axis=-1) @ p['w_out0'][li]
    return out, (h_new.astype(dty), new_conv, kc_n, vc_n, ks_n, vs_n, new_wk, new_wv)


def conformer_mixer(h, li, p, st):
    bt = h.shape[0]
    a, g = split_last(h @ p['conf_w_in'][li], (CONF_CH, CONF_CH))
    u = a * jax.nn.sigmoid(g)
    if st is None:
        buf = jnp.zeros((bt, CONF_KERNEL - 1, CONF_CH), h.dtype)
    else:
        buf = st['conf_conv'][:, li]
    u_pad = jnp.concatenate([buf, u], axis=1)
    new_buf = u_pad[:, -(CONF_KERNEL - 1):]
    c = causal_dwconv(u_pad, p['conf_dw'][li]) + p['conf_dw_b'][li]
    c = jax.nn.silu(layernorm(c, p['conf_ln_g'][li], p['conf_ln_b'][li]))
    return c @ p['conf_w_out'][li], new_buf


def cross_attend(h, mk, mv, wq, wo):
    bt, L, _ = h.shape
    q = (h @ wq).reshape(bt, L, MEM_HEADS, MEM_HEADDIM) * (MEM_HEADDIM ** -0.5)
    s = jnp.einsum('blhd,bmhd->bhlm', q, mk).astype(jnp.float32)
    pr = jax.nn.softmax(s, axis=-1).astype(mv.dtype)
    return jnp.einsum('bhlm,bmhd->blhd', pr, mv).reshape(bt, L, MEM_HEADS * MEM_HEADDIM) @ wo


def dense_swiglu(h, wg, wu, wd):
    return (jax.nn.silu(h @ wg) * (h @ wu)) @ wd


def moe_swiglu(h, w_router, wg, wu, wd):
    logits = (h @ w_router).astype(jnp.float32)
    top_v, top_i = lax.top_k(logits, TOP_K)
    top_w = jax.nn.softmax(top_v, axis=-1)
    gate = jnp.einsum('...k,...ke->...e', top_w, jax.nn.one_hot(top_i, N_EXPERTS, dtype=jnp.float32))
    out = jnp.zeros_like(h)
    for e in range(N_EXPERTS):
        y_e = (jax.nn.silu(h @ wg[e]) * (h @ wu[e])) @ wd[e]
        out = out + (gate[..., e:e + 1] * y_e).astype(h.dtype)
    return out


def run_trunk(x, mem_k, mem_v, p, st):
    even_states = []
    odd_states = []
    for l in range(DEPTH):
        li = l // 2
        h = rmsnorm(x, p['norm_mix'][l])
        if l % 2 == 0:
            mix, s = even_mixer(h, li, p, st)
            even_states.append(s)
        else:
            mix, s = conformer_mixer(h, li, p, st)
            odd_states.append(s)
        x = x + mix
        h = rmsnorm(x, p['norm_cross'][l])
        x = x + cross_attend(h, mem_k[:, l], mem_v[:, l], p['mem_wq'][l], p['mem_wo'][l])
        h = rmsnorm(x, p['norm_ffn'][l])
        if l % 2 == 0:
            x = x + dense_swiglu(h, p['ffn_wg'][li], p['ffn_wu'][li], p['ffn_wd'][li])
        else:
            x = x + moe_swiglu(h, p['moe_router'][li], p['moe_wg'][li], p['moe_wu'][li], p['moe_wd'][li])
    y = rmsnorm(x, p['norm_final'])
    ssm, sconv, ck, cv, sk, sv, wk, wv = [jnp.stack([s[i] for s in even_states], axis=1) for i in range(8)]
    conf = jnp.stack(odd_states, axis=1)
    return y, ssm, sconv, ck, cv, sk, sv, wk, wv, conf


def setup_inputs(seed: int = 0) -> dict:
    key = jax.random.key(seed)
    keys = iter(jax.random.split(key, 64))
    f32 = jnp.float32

    def nrm(shape, scale):
        return jax.random.normal(next(keys), shape, f32) * scale

    n_pages = PAST_LEN // PAGE_SIZE
    n_used = DEC_BATCH * n_pages
    n_phys = n_used + max(1, n_used // 4)
    wb = min(WINDOW, PAST_LEN)
    kvp = (n_phys, N_EVEN, PAGE_SIZE, NSA_KV_GROUPS, NSA_HEADDIM)
    inp = {}
    inp['x_prompt'] = nrm((BATCH, SEQ, D_MODEL), 1.0)
    inp['x_sample'] = nrm((DEC_BATCH, DEC_SEQ, D_MODEL), 1.0)
    inp['mem_prompt'] = nrm((BATCH, MEM_TOKENS, D_MODEL), 1.0)
    inp['cache_mem_k'] = nrm((DEC_BATCH, DEPTH, MEM_TOKENS, MEM_HEADS, MEM_HEADDIM), 1.0)
    inp['cache_mem_v'] = nrm((DEC_BATCH, DEPTH, MEM_TOKENS, MEM_HEADS, MEM_HEADDIM), 1.0)
    inp['cache_nsa_cmp_k'] = nrm(kvp, 1.0)
    inp['cache_nsa_cmp_v'] = nrm(kvp, 1.0)
    inp['cache_nsa_sel_k'] = nrm(kvp, 1.0)
    inp['cache_nsa_sel_v'] = nrm(kvp, 1.0)
    inp['cache_nsa_win_k'] = nrm((DEC_BATCH, N_EVEN, wb, NSA_KV_GROUPS, NSA_HEADDIM), 1.0)
    inp['cache_nsa_win_v'] = nrm((DEC_BATCH, N_EVEN, wb, NSA_KV_GROUPS, NSA_HEADDIM), 1.0)
    inp['state_ssm'] = nrm((DEC_BATCH, N_EVEN, SSD_HEADS, SSD_HEADDIM, SSD_STATE), 0.5)
    inp['state_ssd_conv'] = nrm((DEC_BATCH, N_EVEN, SSD_CONV - 1, SSD_CONV_DIM), 1.0)
    inp['state_conf_conv'] = nrm((DEC_BATCH, N_ODD, CONF_KERNEL - 1, CONF_CH), 0.5)
    inp['page_table'] = jax.random.permutation(next(keys), n_phys)[:n_used].reshape(DEC_BATCH, n_pages).astype(jnp.int32)
    inp['norm_mix'] = 1.0 + nrm((DEPTH, D_MODEL), 0.02)
    inp['norm_cross'] = 1.0 + nrm((DEPTH, D_MODEL), 0.02)
    inp['norm_ffn'] = 1.0 + nrm((DEPTH, D_MODEL), 0.02)
    inp['norm_final'] = 1.0 + nrm((D_MODEL,), 0.02)
    inp['w_in0'] = nrm((N_EVEN, D_MODEL, IN0_DIM), D_MODEL ** -0.5)
    inp['w_out0'] = nrm((N_EVEN, MIX0_DIM, D_MODEL), MIX0_DIM ** -0.5)
    inp['ssd_conv_w'] = nrm((N_EVEN, SSD_CONV, SSD_CONV_DIM), SSD_CONV ** -0.5)
    inp['ssd_conv_b'] = nrm((N_EVEN, SSD_CONV_DIM), 0.02)
    dt_init = jnp.exp(jax.random.uniform(next(keys), (N_EVEN, SSD_HEADS), f32, math.log(1e-3), math.log(1e-1)))
    inp['ssd_dt_bias'] = dt_init + jnp.log(-jnp.expm1(-dt_init))
    inp['ssd_a_log'] = jnp.log(jax.random.uniform(next(keys), (N_EVEN, SSD_HEADS), f32, 1.0, 16.0))
    inp['ssd_d'] = 1.0 + nrm((N_EVEN, SSD_HEADS), 0.02)
    inp['ssd_norm'] = 1.0 + nrm((N_EVEN, SSD_D_INNER), 0.02)
    inp['rel_bias'] = nrm((REL_BUCKETS, NSA_HEADS), 0.5)
    inp['conf_w_in'] = nrm((N_ODD, D_MODEL, 2 * CONF_CH), D_MODEL ** -0.5)
    inp['conf_dw'] = nrm((N_ODD, CONF_KERNEL, CONF_CH), CONF_KERNEL ** -0.5)
    inp['conf_dw_b'] = nrm((N_ODD, CONF_CH), 0.02)
    inp['conf_ln_g'] = 1.0 + nrm((N_ODD, CONF_CH), 0.02)
    inp['conf_ln_b'] = nrm((N_ODD, CONF_CH), 0.02)
    inp['conf_w_out'] = nrm((N_ODD, CONF_CH, D_MODEL), CONF_CH ** -0.5)
    inp['mem_wq'] = nrm((DEPTH, D_MODEL, MEM_HEADS * MEM_HEADDIM), D_MODEL ** -0.5)
    inp['mem_wk'] = nrm((DEPTH, D_MODEL, MEM_HEADS * MEM_HEADDIM), D_MODEL ** -0.5)
    inp['mem_wv'] = nrm((DEPTH, D_MODEL, MEM_HEADS * MEM_HEADDIM), D_MODEL ** -0.5)
    inp['mem_wo'] = nrm((DEPTH, MEM_HEADS * MEM_HEADDIM, D_MODEL), (MEM_HEADS * MEM_HEADDIM) ** -0.5)
    inp['ffn_wg'] = nrm((N_EVEN, D_MODEL, D_FF), D_MODEL ** -0.5)
    inp['ffn_wu'] = nrm((N_EVEN, D_MODEL, D_FF), D_MODEL ** -0.5)
    inp['ffn_wd'] = nrm((N_EVEN, D_FF, D_MODEL), D_FF ** -0.5)
    inp['moe_router'] = nrm((N_ODD, D_MODEL, N_EXPERTS), D_MODEL ** -0.5)
    inp['moe_wg'] = nrm((N_ODD, N_EXPERTS, D_MODEL, D_FF), D_MODEL ** -0.5)
    inp['moe_wu'] = nrm((N_ODD, N_EXPERTS, D_MODEL, D_FF), D_MODEL ** -0.5)
    inp['moe_wd'] = nrm((N_ODD, N_EXPERTS, D_FF, D_MODEL), D_FF ** -0.5)
    return inp


def reference(x_prompt, x_sample, mem_prompt, cache_mem_k, cache_mem_v,
              cache_nsa_cmp_k, cache_nsa_cmp_v, cache_nsa_sel_k, cache_nsa_sel_v,
              cache_nsa_win_k, cache_nsa_win_v, state_ssm, state_ssd_conv, state_conf_conv, page_table,
              norm_mix, norm_cross, norm_ffn, norm_final, w_in0, w_out0,
              ssd_conv_w, ssd_conv_b, ssd_dt_bias, ssd_a_log, ssd_d, ssd_norm, rel_bias,
              conf_w_in, conf_dw, conf_dw_b, conf_ln_g, conf_ln_b, conf_w_out,
              mem_wq, mem_wk, mem_wv, mem_wo, ffn_wg, ffn_wu, ffn_wd,
              moe_router, moe_wg, moe_wu, moe_wd):
    p = {'norm_mix': norm_mix, 'norm_cross': norm_cross, 'norm_ffn': norm_ffn, 'norm_final': norm_final,
         'w_in0': w_in0, 'w_out0': w_out0, 'ssd_conv_w': ssd_conv_w, 'ssd_conv_b': ssd_conv_b,
         'ssd_dt_bias': ssd_dt_bias, 'ssd_a_log': ssd_a_log, 'ssd_d': ssd_d, 'ssd_norm': ssd_norm,
         'rel_bias': rel_bias, 'conf_w_in': conf_w_in, 'conf_dw': conf_dw, 'conf_dw_b': conf_dw_b,
         'conf_ln_g': conf_ln_g, 'conf_ln_b': conf_ln_b, 'conf_w_out': conf_w_out,
         'mem_wq': mem_wq, 'mem_wo': mem_wo, 'ffn_wg': ffn_wg, 'ffn_wu': ffn_wu, 'ffn_wd': ffn_wd,
         'moe_router': moe_router, 'moe_wg': moe_wg, 'moe_wu': moe_wu, 'moe_wd': moe_wd}
    bp = mem_prompt.shape[0]
    mem_k_p = jnp.einsum('bmd,lde->blme', mem_prompt, mem_wk).reshape(bp, DEPTH, MEM_TOKENS, MEM_HEADS, MEM_HEADDIM)
    mem_v_p = jnp.einsum('bmd,lde->blme', mem_prompt, mem_wv).reshape(bp, DEPTH, MEM_TOKENS, MEM_HEADS, MEM_HEADDIM)
    (y_prompt, p_ssm, p_ssd_conv, p_cmp_k, p_cmp_v, p_sel_k, p_sel_v,
     p_win_k, p_win_v, p_conf) = run_trunk(x_prompt, mem_k_p, mem_v_p, p, None)
    st = {'ssm': state_ssm, 'ssd_conv': state_ssd_conv, 'win_k': cache_nsa_win_k, 'win_v': cache_nsa_win_v,
          'conf_conv': state_conf_conv, 'page_table': page_table,
          'pages': (cache_nsa_cmp_k, cache_nsa_cmp_v, cache_nsa_sel_k, cache_nsa_sel_v)}
    (y_sample, s_ssm, s_ssd_conv, s_cmp_k, s_cmp_v, s_sel_k, s_sel_v,
     s_win_k, s_win_v, s_conf) = run_trunk(x_sample, cache_mem_k, cache_mem_v, p, st)
    return (y_prompt, y_sample, mem_k_p, mem_v_p, p_ssm, p_ssd_conv, p_cmp_k, p_cmp_v, p_sel_k, p_sel_v,
            p_win_k, p_win_v, p_conf, s_ssm, s_ssd_conv, s_cmp_k, s_cmp_v, s_sel_k, s_sel_v,
            s_win_k, s_win_v, s_conf)
```

```python
import functools
import math

import numpy as np
import jax
import jax.numpy as jnp
from jax import lax
from jax.experimental import pallas as pl
from jax.experimental.pallas import tpu as pltpu

F32 = jnp.float32
BF16 = jnp.bfloat16

SSD_HEADS = 16
SSD_HEADDIM = 64
SSD_GROUPS = 2
SSD_STATE = 128
SSD_CONV = 4
SSD_CHUNK = 128
NSA_HEADS = 16
NSA_HEADDIM = 64
NSA_GROUPS = 2
NSA_REP = 8
CMP_BLOCK = 32
SEL_BLOCK = 64
SEL_TOPK = 16
WINDOW = 512
Q_BLOCK = 128
REL_BUCKETS = 32
REL_MAX_DIST = 1024
CONF_KERNEL = 31
MEM_HEADS = 4
MEM_HEADDIM = 128
N_EXPERTS = 8
RMS_EPS = 1e-6
LN_EPS = 1e-5
NEG_INF = -1e30
FORCE_SCORE = 1e30

LANES = 128
SUBLANES = 8
VMEM_LIMIT = 56 * 1024 * 1024

COL_Z = 0
COL_X = 1024
COL_BC = 2048
COL_Q = 2560
COL_KV = 3584
COL_MISC = 4352
IN0_PAD = 4608

HI = lax.Precision.HIGHEST


def _cparams(sem):
    return pltpu.CompilerParams(dimension_semantics=sem, vmem_limit_bytes=VMEM_LIMIT)


def _pick(n, pref):
    for t in pref:
        if n % t == 0:
            return t
    return n


def _silu(x):
    return x * (1.0 / (1.0 + jnp.exp(-x)))


def _sigmoid(x):
    return 1.0 / (1.0 + jnp.exp(-x))


def _dot(a, b):
    return jnp.dot(a.astype(BF16), b.astype(BF16), preferred_element_type=F32)


def _dot_nt(a, b):
    return lax.dot_general(a.astype(BF16), b.astype(BF16), (((1,), (1,)), ((), ())),
                           preferred_element_type=F32)


def _dot_tn(a, b):
    return lax.dot_general(a.astype(BF16), b.astype(BF16), (((0,), (0,)), ((), ())),
                           preferred_element_type=F32)


def _dot_hi(a, b):
    return jnp.dot(a, b, precision=HI, preferred_element_type=F32)


def _dot_nt_hi(a, b):
    return lax.dot_general(a, b, (((1,), (1,)), ((), ())), precision=HI, preferred_element_type=F32)


def _iota(shape, dim):
    return lax.broadcasted_iota(jnp.int32, shape, dim)


def _ones_where(cond, dtype):
    return jnp.where(cond, 1.0, 0.0).astype(dtype)


def _rms(x, g):
    return x * lax.rsqrt(jnp.mean(x * x, axis=-1, keepdims=True) + RMS_EPS) * g


def _norm_matmul_kernel(x_ref, g_ref, w_ref, o_ref, xn_ref):
    @pl.when(pl.program_id(1) == 0)
    def _():
        xn_ref[...] = _rms(x_ref[...], g_ref[...]).astype(BF16)

    o_ref[...] = jnp.dot(xn_ref[...], w_ref[...], preferred_element_type=F32)


def norm_matmul(x, g, w):
    m, k = x.shape
    n = w.shape[1]
    tm = _pick(m, (512, 256, 128, 64, 32, 16, 8))
    tn = _pick(n, (512, 256, 128))
    return pl.pallas_call(
        _norm_matmul_kernel,
        grid=(m // tm, n // tn),
        in_specs=[pl.BlockSpec((tm, k), lambda i, j: (i, 0)),
                  pl.BlockSpec((1, k), lambda i, j: (0, 0)),
                  pl.BlockSpec((k, tn), lambda i, j: (0, j))],
        out_specs=pl.BlockSpec((tm, tn), lambda i, j: (i, j)),
        out_shape=jax.ShapeDtypeStruct((m, n), F32),
        scratch_shapes=[pltpu.VMEM((tm, k), BF16)],
        compiler_params=_cparams(("parallel", "arbitrary")),
        name="norm_matmul",
    )(x, g.reshape(1, k), w)


def _matmul_res_kernel(*refs, n_in):
    a_refs = refs[:n_in]
    w_refs = refs[n_in:2 * n_in]
    r_ref, o_ref = refs[2 * n_in], refs[2 * n_in + 1]
    acc = r_ref[...]
    for a_ref, w_ref in zip(a_refs, w_refs):
        acc = acc + jnp.dot(a_ref[...].astype(BF16), w_ref[...], preferred_element_type=F32)
    o_ref[...] = acc


def matmul_res(a_list, w_list, r):
    m, n = r.shape
    tm = _pick(m, (512, 256, 128, 64, 32, 16, 8))
    tn = _pick(n, (512, 256, 128))
    n_in = len(a_list)
    in_specs = ([pl.BlockSpec((tm, a.shape[1]), lambda i, j: (i, 0)) for a in a_list]
                + [pl.BlockSpec((w.shape[0], tn), lambda i, j: (0, j)) for w in w_list]
                + [pl.BlockSpec((tm, tn), lambda i, j: (i, j))])
    return pl.pallas_call(
        functools.partial(_matmul_res_kernel, n_in=n_in),
        grid=(m // tm, n // tn),
        in_specs=in_specs,
        out_specs=pl.BlockSpec((tm, tn), lambda i, j: (i, j)),
        out_shape=jax.ShapeDtypeStruct((m, n), F32),
        compiler_params=_cparams(("parallel", "arbitrary")),
        name="matmul_res",
    )(*a_list, *w_list, r)


def _swiglu_kernel(x_ref, g_ref, wg_ref, wu_ref, wd_ref, o_ref, xn_ref, acc_ref):
    f = pl.program_id(1)

    @pl.when(f == 0)
    def _():
        xn_ref[...] = _rms(x_ref[...], g_ref[...]).astype(BF16)
        acc_ref[...] = jnp.zeros_like(acc_ref)

    xn = xn_ref[...]
    hg = jnp.dot(xn, wg_ref[...], preferred_element_type=F32)
    hu = jnp.dot(xn, wu_ref[...], preferred_element_type=F32)
    acc_ref[...] += jnp.dot((_silu(hg) * hu).astype(BF16), wd_ref[...], preferred_element_type=F32)

    @pl.when(f == pl.num_programs(1) - 1)
    def _():
        o_ref[...] = x_ref[...] + acc_ref[...]


def swiglu_ffn(x, g, wg, wu, wd):
    m, d = x.shape
    ff = wg.shape[1]
    tm = _pick(m, (512, 256, 128, 64, 32, 16, 8))
    tf = _pick(ff, (512, 256, 128))
    return pl.pallas_call(
        _swiglu_kernel,
        grid=(m // tm, ff // tf),
        in_specs=[pl.BlockSpec((tm, d), lambda i, f: (i, 0)),
                  pl.BlockSpec((1, d), lambda i, f: (0, 0)),
                  pl.BlockSpec((d, tf), lambda i, f: (0, f)),
                  pl.BlockSpec((d, tf), lambda i, f: (0, f)),
                  pl.BlockSpec((tf, d), lambda i, f: (f, 0))],
        out_specs=pl.BlockSpec((tm, d), lambda i, f: (i, 0)),
        out_shape=jax.ShapeDtypeStruct((m, d), F32),
        scratch_shapes=[pltpu.VMEM((tm, d), BF16), pltpu.VMEM((tm, d), F32)],
        compiler_params=_cparams(("parallel", "arbitrary")),
        name="swiglu_ffn",
    )(x, g.reshape(1, d), wg, wu, wd)


def _top2_gate(logits):
    lane = _iota(logits.shape, 1)
    m1 = jnp.max(logits, axis=-1, keepdims=True)
    i1 = jnp.min(jnp.where(logits == m1, lane, LANES), axis=-1, keepdims=True)
    rest = jnp.where(lane == i1, -jnp.inf, logits)
    m2 = jnp.max(rest, axis=-1, keepdims=True)
    i2 = jnp.min(jnp.where(rest == m2, lane, LANES), axis=-1, keepdims=True)
    e2 = jnp.exp(m2 - m1)
    den = 1.0 + e2
    return jnp.where(lane == i1, 1.0 / den, 0.0) + jnp.where(lane == i2, e2 / den, 0.0)


def _moe_kernel(x_ref, g_ref, wr_ref, wg_ref, wu_ref, wd_ref, gf_ref, o_ref, xn_ref, gate_ref, acc_ref):
    e = pl.program_id(1)
    f = pl.program_id(2)

    @pl.when((e == 0) & (f == 0))
    def _():
        xn = _rms(x_ref[...], g_ref[...]).astype(BF16)
        xn_ref[...] = xn
        logits = jnp.dot(xn, wr_ref[...], preferred_element_type=F32)
        lane = _iota(logits.shape, 1)
        gate_ref[...] = _top2_gate(jnp.where(lane < N_EXPERTS, logits, -jnp.inf))
        acc_ref[...] = jnp.zeros_like(acc_ref)

    xn = xn_ref[...]
    gate = gate_ref[...]
    lane = _iota(gate.shape, 1)
    ge = jnp.sum(jnp.where(lane == e, gate, 0.0), axis=-1, keepdims=True)
    hg = jnp.dot(xn, wg_ref[0], preferred_element_type=F32)
    hu = jnp.dot(xn, wu_ref[0], preferred_element_type=F32)
    y = jnp.dot((_silu(hg) * hu).astype(BF16), wd_ref[0], preferred_element_type=F32)
    acc_ref[...] += ge * y

    @pl.when((e == pl.num_programs(1) - 1) & (f == pl.num_programs(2) - 1))
    def _():
        o_ref[...] = _rms(x_ref[...] + acc_ref[...], gf_ref[...])


def moe_ffn_final(x, g, w_router, wg, wu, wd, g_final):
    m, d = x.shape
    ne, _, ff = wg.shape
    tm = _pick(m, (512, 256, 128, 64, 32, 16, 8))
    tf = _pick(ff, (512, 256, 128))
    wr = jnp.zeros((d, LANES), BF16).at[:, :ne].set(w_router.astype(BF16))
    return pl.pallas_call(
        _moe_kernel,
        grid=(m // tm, ne, ff // tf),
        in_specs=[pl.BlockSpec((tm, d), lambda i, e, f: (i, 0)),
                  pl.BlockSpec((1, d), lambda i, e, f: (0, 0)),
                  pl.BlockSpec((d, LANES), lambda i, e, f: (0, 0)),
                  pl.BlockSpec((1, d, tf), lambda i, e, f: (e, 0, f)),
                  pl.BlockSpec((1, d, tf), lambda i, e, f: (e, 0, f)),
                  pl.BlockSpec((1, tf, d), lambda i, e, f: (e, f, 0)),
                  pl.BlockSpec((1, d), lambda i, e, f: (0, 0))],
        out_specs=pl.BlockSpec((tm, d), lambda i, e, f: (i, 0)),
        out_shape=jax.ShapeDtypeStruct((m, d), F32),
        scratch_shapes=[pltpu.VMEM((tm, d), BF16), pltpu.VMEM((tm, LANES), F32), pltpu.VMEM((tm, d), F32)],
        compiler_params=_cparams(("parallel", "arbitrary", "arbitrary")),
        name="moe_ffn",
    )(x, g.reshape(1, d), wr, wg, wu, wd, g_final.reshape(1, d))


def _mem_kv_kernel(x_ref, wk_ref, wv_ref, k_ref, v_ref):
    x = x_ref[0].astype(BF16)
    k_ref[0, 0] = jnp.dot(x, wk_ref[0], preferred_element_type=F32)
    v_ref[0, 0] = jnp.dot(x, wv_ref[0], preferred_element_type=F32)


def mem_kv(mem, wk, wv):
    b, t, d = mem.shape
    nl, _, e = wk.shape
    out = jax.ShapeDtypeStruct((b, nl, t, e), F32)
    return pl.pallas_call(
        _mem_kv_kernel,
        grid=(b, nl),
        in_specs=[pl.BlockSpec((1, t, d), lambda i, l: (i, 0, 0)),
                  pl.BlockSpec((1, d, e), lambda i, l: (l, 0, 0)),
                  pl.BlockSpec((1, d, e), lambda i, l: (l, 0, 0))],
        out_specs=[pl.BlockSpec((1, 1, t, e), lambda i, l: (i, l, 0, 0)),
                   pl.BlockSpec((1, 1, t, e), lambda i, l: (i, l, 0, 0))],
        out_shape=[out, out],
        compiler_params=_cparams(("parallel", "arbitrary")),
        name="mem_kv",
    )(mem, wk, wv)


def _cross_attn_kernel(q_ref, k_ref, v_ref, o_ref):
    q = q_ref[0] * (MEM_HEADDIM ** -0.5)
    k = k_ref[0, 0]
    v = v_ref[0, 0]
    outs = []
    for h in range(MEM_HEADS):
        sl = slice(h * MEM_HEADDIM, (h + 1) * MEM_HEADDIM)
        s = _dot_nt(q[:, sl], k[:, sl])
        p = jnp.exp(s - jnp.max(s, axis=-1, keepdims=True))
        p = p / jnp.sum(p, axis=-1, keepdims=True)
        outs.append(_dot(p, v[:, sl]))
    o_ref[0] = jnp.concatenate(outs, axis=-1)


def cross_attn_core(q, mk, mv, layer):
    s, t, e = q.shape
    mt = mk.shape[2]
    tq = _pick(t, (512, 256, 128)) if t >= 128 else t
    return pl.pallas_call(
        _cross_attn_kernel,
        grid=(s, t // tq),
        in_specs=[pl.BlockSpec((1, tq, e), lambda i, j: (i, j, 0)),
                  pl.BlockSpec((1, 1, mt, e), lambda i, j: (i, layer, 0, 0)),
                  pl.BlockSpec((1, 1, mt, e), lambda i, j: (i, layer, 0, 0))],
        out_specs=pl.BlockSpec((1, tq, e), lambda i, j: (i, j, 0)),
        out_shape=jax.ShapeDtypeStruct((s, t, e), F32),
        compiler_params=_cparams(("parallel", "arbitrary")),
        name="cross_attn",
    )(q, mk, mv)


def cross_attend(x, nseq, g, wq, wo, mk, mv, layer):
    m, d = x.shape
    e = wq.shape[1]
    q = norm_matmul(x, g, wq)
    mk = mk.reshape(mk.shape[0], mk.shape[1], mk.shape[2], e)
    mv = mv.reshape(mk.shape)
    o = cross_attn_core(q.reshape(nseq, m // nseq, e), mk, mv, layer)
    return matmul_res([o.reshape(m, e)], [wo], x)


def _ssd_kernel(*refs, lreal, has_state):
    if has_state:
        (z_ref, x_ref, bc_ref, dt_ref, hist_ref, h0_ref, cw_ref, cb_ref, dtb_ref, alog_ref, d_ref, nw_ref,
         y_ref, hout_ref, xp_scr, st_scr, y_scr) = refs
    else:
        (z_ref, x_ref, bc_ref, dt_ref, cw_ref, cb_ref, dtb_ref, alog_ref, d_ref, nw_ref,
         y_ref, hout_ref, xp_scr, st_scr, y_scr) = refs
    c = pl.program_id(1)
    q8 = xp_scr.shape[0] - SUBLANES
    dx = x_ref.shape[-1]
    n = SSD_STATE
    hist = SSD_CONV - 1

    @pl.when(c == 0)
    def _():
        xp_scr[...] = jnp.zeros_like(xp_scr)
        if has_state:
            xp_scr[SUBLANES - hist:SUBLANES, :] = hist_ref[0, 0]
            st_scr[...] = h0_ref[0]
        else:
            st_scr[...] = jnp.zeros_like(st_scr)

    xp_scr[SUBLANES:SUBLANES + lreal, :dx] = x_ref[0]
    xp_scr[SUBLANES:SUBLANES + lreal, dx:] = bc_ref[0]
    conv = cb_ref[...]
    for k in range(SSD_CONV):
        lo = SUBLANES - hist + k
        conv = conv + cw_ref[k:k + 1, :] * xp_scr[lo:lo + q8, :]
    xc = _silu(conv)
    if q8 == lreal:
        xp_scr[0:SUBLANES, :] = xp_scr[q8:q8 + SUBLANES, :]

    lane = _iota((q8, LANES), 1)
    row = _iota((q8, LANES), 0)
    dt = jax.nn.softplus(dt_ref[0] + dtb_ref[...]) if q8 == lreal else None
    if dt is None:
        dtr = jnp.concatenate([dt_ref[0], jnp.zeros((q8 - lreal, LANES), F32)], axis=0)
        dt = jax.nn.softplus(dtr + dtb_ref[...])
    dt = jnp.where((lane < SSD_HEADS) & (row < lreal), dt, 0.0)
    ad = dt * (-jnp.exp(alog_ref[...]))
    tri = (_iota((q8, q8), 0) >= _iota((q8, q8), 1))
    acum = _dot_hi(tri.astype(F32), ad)
    eye = (_iota((LANES, LANES), 0) == _iota((LANES, LANES), 1)).astype(F32)
    acum_t = _dot_nt_hi(eye, acum)
    alast = acum[q8 - 1:q8, :]
    lo_half = lane < SSD_HEADDIM

    for g in range(SSD_GROUPS):
        bg = xc[:, dx + g * n:dx + (g + 1) * n]
        cg = xc[:, dx + (SSD_GROUPS + g) * n:dx + (SSD_GROUPS + g + 1) * n]
        cb = _dot_nt(cg, bg)
        hpg = SSD_HEADS // SSD_GROUPS
        for jp in range(hpg // 2):
            pair = g * (hpg // 2) + jp
            ha, hb = 2 * pair, 2 * pair + 1

            def sel(col_a, col_b):
                return jnp.where(lo_half, col_a, col_b)

            xs = xc[:, pair * LANES:(pair + 1) * LANES]
            xd = xs * sel(dt[:, ha:ha + 1], dt[:, hb:hb + 1])
            ys = []
            for h in (ha, hb):
                diff = acum[:, h:h + 1] - acum_t[h:h + 1, :]
                lmat = jnp.exp(jnp.where(tri, diff, NEG_INF))
                ys.append(_dot(cb * lmat, xd))
            y_diag = jnp.where(lo_half, ys[0], ys[1])
            st = st_scr[pair]
            y_off = _dot(cg, st) * sel(jnp.exp(acum[:, ha:ha + 1]), jnp.exp(acum[:, hb:hb + 1]))
            ds = sel(jnp.exp(alast[:, ha:ha + 1] - acum[:, ha:ha + 1]),
                     jnp.exp(alast[:, hb:hb + 1] - acum[:, hb:hb + 1]))
            dec = jnp.where(lo_half[0:1], jnp.exp(alast[:, ha:ha + 1]), jnp.exp(alast[:, hb:hb + 1]))
            st_scr[pair] = dec * st + _dot_tn(bg, xd * ds)
            dsk = jnp.where(lo_half[0:1], d_ref[:, ha:ha + 1], d_ref[:, hb:hb + 1])
            y_scr[:, pair * LANES:(pair + 1) * LANES] = y_diag + y_off + xs * dsk

    y = y_scr[...]
    if q8 != lreal:
        z = jnp.concatenate([z_ref[0], jnp.zeros((q8 - lreal, dx), F32)], axis=0)
    else:
        z = z_ref[0]
    y = y * _silu(z)
    half = dx // SSD_GROUPS
    lane_x = _iota((q8, dx), 1)
    ms0 = jnp.mean(jnp.square(y[:, :half]), axis=-1, keepdims=True)
    ms1 = jnp.mean(jnp.square(y[:, half:]), axis=-1, keepdims=True)
    scale = jnp.where(lane_x < half, lax.rsqrt(ms0 + RMS_EPS), lax.rsqrt(ms1 + RMS_EPS))
    y_ref[0] = (y * scale * nw_ref[...])[:lreal]

    @pl.when(c == pl.num_programs(1) - 1)
    def _():
        hout_ref[0] = st_scr[...]


def _pad_lanes(v):
    return jnp.zeros((1, LANES), F32).at[0, :v.shape[0]].set(v.astype(F32))


def ssd_mixer(p3, nseq, conv_hist, h0, cw, cb, dt_bias, a_log, d_skip, norm_w):
    nblk, q, _ = p3.shape
    nc = nblk // nseq
    dx = SSD_HEADS * SSD_HEADDIM
    dbc = 2 * SSD_GROUPS * SSD_STATE
    q8 = -(-q // SUBLANES) * SUBLANES
    npair = SSD_HEADS // 2
    has_state = h0 is not None
    row = lambda s, c: (s * nc + c, 0, 0)
    in_specs = [pl.BlockSpec((1, q, dx), lambda s, c: (s * nc + c, 0, COL_Z // dx)),
                pl.BlockSpec((1, q, dx), lambda s, c: (s * nc + c, 0, COL_X // dx)),
                pl.BlockSpec((1, q, dbc), lambda s, c: (s * nc + c, 0, COL_BC // dbc)),
                pl.BlockSpec((1, q, LANES), lambda s, c: (s * nc + c, 0, COL_MISC // LANES))]
    args = [p3, p3, p3, p3]
    if has_state:
        in_specs += [pl.BlockSpec((1, 1, SSD_CONV - 1, dx + dbc), lambda s, c: (s, 0, 0, 0)),
                     pl.BlockSpec((1, npair, SSD_STATE, LANES), lambda s, c: (s, 0, 0, 0))]
        args += [conv_hist, h0]
    const = lambda s, c: (0, 0)
    in_specs += [pl.BlockSpec((SSD_CONV, dx + dbc), const), pl.BlockSpec((1, dx + dbc), const),
                 pl.BlockSpec((1, LANES), const), pl.BlockSpec((1, LANES), const),
                 pl.BlockSpec((1, LANES), const), pl.BlockSpec((1, dx), const)]
    args += [cw, cb.reshape(1, -1), _pad_lanes(dt_bias), _pad_lanes(a_log), _pad_lanes(d_skip),
             norm_w.reshape(1, dx)]
    del row
    return pl.pallas_call(
        functools.partial(_ssd_kernel, lreal=q, has_state=has_state),
        grid=(nseq, nc),
        in_specs=in_specs,
        out_specs=[pl.BlockSpec((1, q, dx), lambda s, c: (s * nc + c, 0, 0)),
                   pl.BlockSpec((1, npair, SSD_STATE, LANES), lambda s, c: (s, 0, 0, 0))],
        out_shape=[jax.ShapeDtypeStruct((nblk, q, dx), F32),
                   jax.ShapeDtypeStruct((nseq, npair, SSD_STATE, LANES), F32)],
        scratch_shapes=[pltpu.VMEM((SUBLANES + q8, dx + dbc), F32),
                        pltpu.VMEM((npair, SSD_STATE, LANES), F32),
                        pltpu.VMEM((q8, dx), F32)],
        compiler_params=_cparams(("parallel", "arbitrary")),
        name="ssd_mixer",
    )(*args)


def _state_to_pairs(h):
    s, nh, p, n = h.shape
    return h.reshape(s, nh // 2, 2, p, n).transpose(0, 1, 4, 2, 3).reshape(s, nh // 2, n, 2 * p)


def _pairs_to_state(hp):
    s, npair, n, pp = hp.shape
    return hp.reshape(s, npair, n, 2, pp // 2).transpose(0, 1, 3, 4, 2).reshape(s, 2 * npair, pp // 2, n)


def _reorder_in0(w):
    d = w.shape[0]
    dz = SSD_HEADS * SSD_HEADDIM
    dxbc = dz + 2 * SSD_GROUPS * SSD_STATE
    o_dt = dz + dxbc
    o_q = o_dt + SSD_HEADS
    o_kv = o_q + NSA_HEADS * NSA_HEADDIM
    o_g = o_kv + 6 * NSA_GROUPS * NSA_HEADDIM
    parts = [w[:, :o_dt], w[:, o_q:o_g], w[:, o_dt:o_q], w[:, o_g:]]
    wn = jnp.concatenate(parts, axis=1)
    return jnp.pad(wn, ((0, 0), (0, IN0_PAD - wn.shape[1]))).astype(BF16)


CONF_HALO = 32


def _conf_kernel(*refs, lreal, has_state):
    if has_state:
        ag_ref, x_ref, hist_ref, dw_ref, dwb_ref, lg_ref, lb_ref, wo_ref, o_ref, buf_ref, up_scr = refs
    else:
        ag_ref, x_ref, dw_ref, dwb_ref, lg_ref, lb_ref, wo_ref, o_ref, buf_ref, up_scr = refs
    c = pl.program_id(1)
    ch = x_ref.shape[-1]
    t8 = up_scr.shape[0] - CONF_HALO
    nh = CONF_KERNEL - 1

    @pl.when(c == 0)
    def _():
        up_scr[...] = jnp.zeros_like(up_scr)
        if has_state:
            up_scr[CONF_HALO - nh:CONF_HALO, :] = hist_ref[0, 0]

    ag = ag_ref[0]
    up_scr[CONF_HALO:CONF_HALO + lreal, :] = ag[:, :ch] * _sigmoid(ag[:, ch:])
    acc = dwb_ref[...] + dw_ref[0:1, :] * up_scr[CONF_HALO - nh:CONF_HALO - nh + t8, :]
    for k in range(1, CONF_KERNEL):
        lo = CONF_HALO - nh + k
        acc = acc + dw_ref[k:k + 1, :] * up_scr[lo:lo + t8, :]
    mu = jnp.mean(acc, axis=-1, keepdims=True)
    cen = acc - mu
    var = jnp.mean(cen * cen, axis=-1, keepdims=True)
    y = _silu(cen * lax.rsqrt(var + LN_EPS) * lg_ref[...] + lb_ref[...])
    o_ref[0] = x_ref[0] + jnp.dot(y.astype(BF16), wo_ref[...], preferred_element_type=F32)[:lreal]

    @pl.when(c == pl.num_programs(1) - 1)
    def _():
        buf_ref[0] = up_scr[CONF_HALO + lreal - nh:CONF_HALO + lreal, :]

    if t8 == lreal:
        up_scr[0:CONF_HALO, :] = up_scr[t8:t8 + CONF_HALO, :]


def conf_mixer(ag3, x3, nseq, hist, dw, dwb, ln_g, ln_b, w_out):
    nblk, t, ch = x3.shape
    nc = nblk // nseq
    t8 = -(-t // SUBLANES) * SUBLANES
    nh = CONF_KERNEL - 1
    has_state = hist is not None
    const = lambda s, c: (0, 0)
    in_specs = [pl.BlockSpec((1, t, 2 * ch), lambda s, c: (s * nc + c, 0, 0)),
                pl.BlockSpec((1, t, ch), lambda s, c: (s * nc + c, 0, 0))]
    args = [ag3, x3]
    if has_state:
        in_specs.append(pl.BlockSpec((1, 1, nh, ch), lambda s, c: (s, 0, 0, 0)))
        args.append(hist)
    in_specs += [pl.BlockSpec((CONF_KERNEL, ch), const), pl.BlockSpec((1, ch), const),
                 pl.BlockSpec((1, ch), const), pl.BlockSpec((1, ch), const), pl.BlockSpec((ch, ch), const)]
    args += [dw, dwb.reshape(1, ch), ln_g.reshape(1, ch), ln_b.reshape(1, ch), w_out]
    return pl.pallas_call(
        functools.partial(_conf_kernel, lreal=t, has_state=has_state),
        grid=(nseq, nc),
        in_specs=in_specs,
        out_specs=[pl.BlockSpec((1, t, ch), lambda s, c: (s * nc + c, 0, 0)),
                   pl.BlockSpec((1, nh, ch), lambda s, c: (s, 0, 0))],
        out_shape=[jax.ShapeDtypeStruct((nblk, t, ch), F32), jax.ShapeDtypeStruct((nseq, nh, ch), F32)],
        scratch_shapes=[pltpu.VMEM((CONF_HALO + t8, ch), F32)],
        compiler_params=_cparams(("parallel", "arbitrary")),
        name="conf_mixer",
    )(*args)


def _rel_bucket_np(dist):
    n = np.maximum(dist, 0)
    max_exact = REL_BUCKETS // 2
    nf = np.maximum(n, max_exact).astype(np.float32)
    ratio = np.log(nf / np.float32(max_exact)) / np.float32(math.log(REL_MAX_DIST / max_exact))
    large = max_exact + (ratio * np.float32(REL_BUCKETS - max_exact)).astype(np.int32)
    return np.where(n < max_exact, n, np.minimum(large, REL_BUCKETS - 1)).astype(np.int32)


_FAR_DIST = int(np.argmax(_rel_bucket_np(np.arange(4 * REL_MAX_DIST)) == REL_BUCKETS - 1))
TOEP_TILES = -(-(_FAR_DIST + Q_BLOCK - 1) // Q_BLOCK) + 1
CMP_NEAR = 32
assert CMP_BLOCK * (CMP_NEAR - 3) - (Q_BLOCK - 1) >= _FAR_DIST


def _split3(x):
    hi = x.astype(BF16)
    r1 = x - hi.astype(F32)
    mid = r1.astype(BF16)
    lo = (r1 - mid.astype(F32)).astype(BF16)
    return hi, mid, lo


def _topk_mask(score, k):
    nl = score.shape[1]
    lane = _iota(score.shape, 1)
    sel = jnp.zeros(score.shape, F32)
    for _ in range(k):
        m = jnp.max(score, axis=-1, keepdims=True)
        idx = jnp.min(jnp.where(score == m, lane, nl), axis=-1, keepdims=True)
        pick = lane == idx
        sel = jnp.where(pick, 1.0, sel)
        score = jnp.where(pick, -jnp.inf, score)
    return sel


def _softmax_rows(s):
    p = jnp.exp(s - jnp.max(s, axis=-1, keepdims=True))
    return p * (1.0 / jnp.sum(p, axis=-1, keepdims=True))


PREP_ROWS = 512


def _nsa_prep_kernel(q0_ref, q1_ref, kvc_ref, kvs_ref, kvw_ref, qr_ref, kcm_ref, vcm_ref, ks_ref, vs_ref,
                     kw_ref, vw_ref):
    d = NSA_HEADDIM
    scale = d ** -0.5
    for g, q_ref in enumerate((q0_ref, q1_ref)):
        q = q_ref[0]
        for r in range(NSA_REP):
            qr_ref[0, g, r] = (q[:, r * d:(r + 1) * d] * scale).astype(BF16)
    kvc = kvc_ref[0]
    t = kvc.shape[0]
    means = jnp.mean(kvc.reshape(t // CMP_BLOCK, CMP_BLOCK, kvc.shape[1]), axis=1)
    kvs = kvs_ref[0]
    kvw = kvw_ref[0]
    for g in range(NSA_GROUPS):
        kcm_ref[0, g] = means[:, g * d:(g + 1) * d].astype(BF16)
        vcm_ref[0, g] = means[:, (NSA_GROUPS + g) * d:(NSA_GROUPS + g + 1) * d].astype(BF16)
        ks_ref[0, g] = kvs[:, g * d:(g + 1) * d].astype(BF16)
        vs_ref[0, g] = kvs[:, (NSA_GROUPS + g) * d:(NSA_GROUPS + g + 1) * d].astype(BF16)
        kw_ref[0, g] = kvw[:, g * d:(g + 1) * d].astype(BF16)
        vw_ref[0, g] = kvw[:, (NSA_GROUPS + g) * d:(NSA_GROUPS + g + 1) * d].astype(BF16)


def nsa_prep(p3, nseq):
    nblk, t, _ = p3.shape
    nt = nblk // nseq
    L = nt * t
    d = NSA_HEADDIM
    g = NSA_GROUPS
    qw = NSA_REP * d
    kvw = 2 * g * d
    blk = lambda width, col: pl.BlockSpec((1, t, width), lambda s, c: (s * nt + c, 0, col // width))
    seq4 = lambda rows: pl.BlockSpec((1, g, rows, d), lambda s, c: (s, 0, c, 0))
    return pl.pallas_call(
        _nsa_prep_kernel,
        grid=(nseq, nt),
        in_specs=[blk(qw, COL_Q), blk(qw, COL_Q + qw), blk(kvw, COL_KV), blk(kvw, COL_KV + kvw),
                  blk(kvw, COL_KV + 2 * kvw)],
        out_specs=[pl.BlockSpec((1, g, NSA_REP, t, d), lambda s, c: (s, 0, 0, c, 0)),
                   seq4(t // CMP_BLOCK), seq4(t // CMP_BLOCK), seq4(t), seq4(t), seq4(t), seq4(t)],
        out_shape=[jax.ShapeDtypeStruct((nseq, g, NSA_REP, L, d), BF16),
                   jax.ShapeDtypeStruct((nseq, g, L // CMP_BLOCK, d), BF16),
                   jax.ShapeDtypeStruct((nseq, g, L // CMP_BLOCK, d), BF16)]
                  + [jax.ShapeDtypeStruct((nseq, g, L, d), BF16)] * 4,
        compiler_params=_cparams(("parallel", "arbitrary")),
        name="nsa_prep",
    )(p3, p3, p3, p3, p3)


SEL_KEYS = 512
WIN_KEYS = WINDOW + Q_BLOCK


def _nsa_prompt_kernel(qr_ref, kcm_ref, vcm_ref, ks_ref, vs_ref, kw_ref, vw_ref, misc_ref, t4_ref, toep_ref,
                       o_ref, m_scr, l_scr, acc_scr):
    g = pl.program_id(1)
    qi = pl.program_id(2)
    tq = Q_BLOCK
    rows = NSA_REP * tq
    d = NSA_HEADDIM
    ncb = kcm_ref.shape[2]
    nsb = ncb * CMP_BLOCK // SEL_BLOCK
    L = ks_ref.shape[2]
    q2 = qr_ref[0, 0].reshape(rows, d)
    tok = _iota((rows, 1), 0) % tq
    qpos = qi * tq + tok

    s = _dot_nt(q2, kcm_ref[0, 0])
    kk = t4_ref.shape[2]
    e_row = _iota((kk, ncb), 0)
    c_col = _iota((kk, ncb), 1)
    near = 3 * CMP_NEAR
    shift = (e_row % CMP_NEAR) + c_col == qi * (tq // CMP_BLOCK) + 3
    p4 = _ones_where(((e_row < near) & shift) | ((e_row >= near) & (e_row < near + 3)), BF16)
    s = s + jnp.dot(t4_ref[0], p4, preferred_element_type=F32)
    cmp_end = _iota((rows, ncb), 1) * CMP_BLOCK + (CMP_BLOCK - 1)
    valid_c = qpos >= cmp_end
    s = jnp.where(valid_c, s, NEG_INF)
    p = _softmax_rows(s)
    p = jnp.where(valid_c, p, 0.0)
    o_c = _dot(p, vcm_ref[0, 0])
    psum = jnp.sum(p.reshape(NSA_REP, tq, ncb), axis=0)
    ratio = SEL_BLOCK // CMP_BLOCK
    fold = (_iota((ncb, nsb), 0) // ratio == _iota((ncb, nsb), 1)).astype(F32)
    imp = _dot_hi(psum, fold)

    jj = _iota((tq, nsb), 1)
    cur = (qi * tq + _iota((tq, nsb), 0)) // SEL_BLOCK
    forced = (jj == 0) | (jj == cur) | (jj == cur - 1)
    score = jnp.where(forced, FORCE_SCORE, jnp.where(jj <= cur, imp, NEG_INF))
    sel = _topk_mask(score, min(SEL_TOPK, nsb)).astype(BF16)

    m_scr[...] = jnp.full(m_scr.shape, NEG_INF, F32)
    l_scr[...] = jnp.zeros(l_scr.shape, F32)
    acc_scr[...] = jnp.zeros(acc_scr.shape, F32)
    tk = min(SEL_KEYS, L)
    sub = tk // tq
    qpos_t = qi * tq + _iota((tq, tk), 0)

    def sel_step(kt, carry):
        k0 = pl.multiple_of(kt * tk, tk)
        sc = _dot_nt(q2, ks_ref[0, 0, pl.ds(k0, tk), :])
        parts = []
        for u in range(sub):
            delta = jnp.clip(qi - (kt * sub + u), 0, TOEP_TILES - 1)
            parts.append(sc[:, u * tq:(u + 1) * tq] + toep_ref[0, delta])
        sc = jnp.concatenate(parts, axis=1) if sub > 1 else parts[0]
        kpos = k0 + _iota((tq, tk), 1)
        expand = _ones_where(_iota((nsb, tk), 0) == (k0 + _iota((nsb, tk), 1)) // SEL_BLOCK, BF16)
        chosen = jnp.dot(sel, expand, preferred_element_type=F32)
        keep = (chosen > 0.5) & (kpos <= qpos_t)
        sc = jnp.where(keep[None], sc.reshape(NSA_REP, tq, tk), NEG_INF).reshape(rows, tk)
        m_old = m_scr[...]
        m_new = jnp.maximum(m_old, jnp.max(sc, axis=-1, keepdims=True))
        alpha = jnp.exp(m_old - m_new)
        pexp = jnp.exp(sc - m_new)
        pexp = jnp.where(keep[None], pexp.reshape(NSA_REP, tq, tk), 0.0).reshape(rows, tk)
        l_scr[...] = alpha * l_scr[...] + jnp.sum(pexp, axis=-1, keepdims=True)
        acc_scr[...] = alpha * acc_scr[...] + _dot(pexp, vs_ref[0, 0, pl.ds(k0, tk), :])
        m_scr[...] = m_new
        return carry

    lax.fori_loop(0, (qi * tq) // tk + 1, sel_step, 0)
    o_s = acc_scr[...] * (1.0 / l_scr[...])

    wk = min(WIN_KEYS, L)
    blk0 = jnp.maximum(qi - (wk // tq - 1), 0)
    w0 = pl.multiple_of(blk0 * tq, tq)
    sw = _dot_nt(q2, kw_ref[0, 0, pl.ds(w0, wk), :])
    parts = []
    for u in range(wk // tq):
        delta = jnp.clip(qi - (blk0 + u), 0, TOEP_TILES - 1)
        parts.append(sw[:, u * tq:(u + 1) * tq] + toep_ref[0, delta])
    sw = jnp.concatenate(parts, axis=1)
    dist = qpos - (w0 + _iota((rows, wk), 1))
    sw = jnp.where((dist >= 0) & (dist < WINDOW), sw, NEG_INF)
    o_w = _dot(_softmax_rows(sw), vw_ref[0, 0, pl.ds(w0, wk), :])

    gates = _sigmoid(misc_ref[0])
    lane = _iota((tq, LANES), 1)

    def gate_col(b, r):
        idx = SSD_HEADS + b * NSA_HEADS + g * NSA_REP + r
        return jnp.sum(jnp.where(lane == idx, gates, 0.0), axis=-1, keepdims=True)

    o_c3 = o_c.reshape(NSA_REP, tq, d)
    o_s3 = o_s.reshape(NSA_REP, tq, d)
    o_w3 = o_w.reshape(NSA_REP, tq, d)
    for r in range(NSA_REP):
        o_ref[0, 0, r] = gate_col(0, r) * o_c3[r] + gate_col(1, r) * o_s3[r] + gate_col(2, r) * o_w3[r]


def _prompt_bias_tables(rel_bias):
    tq = Q_BLOCK
    i = np.arange(tq)
    e = np.arange(CMP_NEAR)
    dist_c = i[:, None] + CMP_BLOCK * e[None, :] - (tq - 1)
    bkt_c = _rel_bucket_np(dist_c)
    rel = rel_bias.astype(F32)
    far = rel[REL_BUCKETS - 1]
    near = rel[bkt_c] - far
    near = near.transpose(2, 0, 1).reshape(NSA_GROUPS, NSA_REP * tq, CMP_NEAR)
    farc = jnp.broadcast_to(far.reshape(NSA_GROUPS, NSA_REP, 1, 1), (NSA_GROUPS, NSA_REP, tq, 1))
    farc = farc.reshape(NSA_GROUPS, NSA_REP * tq, 1)
    cols = list(_split3(near)) + list(_split3(farc))
    t4 = jnp.concatenate(cols, axis=-1)
    t4 = jnp.pad(t4, ((0, 0), (0, 0), (0, LANES - t4.shape[-1])))
    dl = np.arange(TOEP_TILES)
    dist_t = tq * dl[:, None, None] + i[None, :, None] - i[None, None, :]
    toep = rel[_rel_bucket_np(dist_t)]
    toep = toep.transpose(3, 0, 1, 2).reshape(NSA_GROUPS, NSA_REP, TOEP_TILES, tq, tq)
    toep = toep.transpose(0, 2, 1, 3, 4).reshape(NSA_GROUPS, TOEP_TILES, NSA_REP * tq, tq)
    return t4, toep


def nsa_prompt(p_rows, nseq, rel_bias):
    m = p_rows.shape[0]
    L = m // nseq
    tq = Q_BLOCK
    d = NSA_HEADDIM
    g = NSA_GROUPS
    prep_rows = min(PREP_ROWS, L)
    qr, kcm, vcm, ks, vs, kw, vw = nsa_prep(p_rows.reshape(m // prep_rows, prep_rows, IN0_PAD), nseq)
    t4, toep = _prompt_bias_tables(rel_bias)
    nq = L // tq
    ncb = L // CMP_BLOCK
    rows = NSA_REP * tq
    full = lambda n: pl.BlockSpec((1, 1, n, d), lambda s, gg, q: (s, gg, 0, 0))
    o = pl.pallas_call(
        _nsa_prompt_kernel,
        grid=(nseq, g, nq),
        in_specs=[pl.BlockSpec((1, 1, NSA_REP, tq, d), lambda s, gg, q: (s, gg, 0, q, 0)),
                  full(ncb), full(ncb), full(L), full(L), full(L), full(L),
                  pl.BlockSpec((1, tq, LANES), lambda s, gg, q: (s * nq + q, 0, COL_MISC // LANES)),
                  pl.BlockSpec((1, rows, LANES), lambda s, gg, q: (gg, 0, 0)),
                  pl.BlockSpec((1, TOEP_TILES, rows, tq), lambda s, gg, q: (gg, 0, 0, 0))],
        out_specs=pl.BlockSpec((1, 1, NSA_REP, tq, d), lambda s, gg, q: (s, gg, 0, q, 0)),
        out_shape=jax.ShapeDtypeStruct((nseq, g, NSA_REP, L, d), F32),
        scratch_shapes=[pltpu.VMEM((rows, 1), F32), pltpu.VMEM((rows, 1), F32), pltpu.VMEM((rows, d), F32)],
        compiler_params=_cparams(("parallel", "parallel", "arbitrary")),
        name="nsa_prompt",
    )(qr, kcm, vcm, ks, vs, kw, vw, p_rows.reshape(m // tq, tq, IN0_PAD), t4, toep)
    return o.transpose(0, 3, 1, 2, 4).reshape(m, g * NSA_REP * d)


SAMPLE_PAD = 128


def _nsa_sample_kernel(*refs, n_pages, t_new):
    pt_ref = refs[0]
    del pt_ref
    q_ref, gate_ref, kvs_ref, kvw_ref, wk_ref, wv_ref, bc_ref, bs_ref, bw_ref, ex_ref = refs[1:11]
    pages = refs[11:11 + 4 * n_pages]
    o_ref, nwk_ref, nwv_ref = refs[11 + 4 * n_pages:14 + 4 * n_pages]
    kc_scr, vc_scr, ks_scr, vs_scr, kw_scr, vw_scr = refs[14 + 4 * n_pages:]
    ck_pages = pages[0:n_pages]
    cv_pages = pages[n_pages:2 * n_pages]
    sk_pages = pages[2 * n_pages:3 * n_pages]
    sv_pages = pages[3 * n_pages:4 * n_pages]
    page = ck_pages[0].shape[2]
    past = n_pages * page
    wb = wk_ref.shape[2]
    nrow = q_ref.shape[1]
    per_page = page // CMP_BLOCK
    ncb = n_pages * per_page
    kd = NSA_GROUPS * NSA_HEADDIM

    def pad_tile(new):
        return jnp.concatenate([new, jnp.zeros((SAMPLE_PAD - t_new, kd), F32)], axis=0).astype(BF16)

    for pg in range(n_pages):
        ck = ck_pages[pg][0, 0]
        cv = cv_pages[pg][0, 0]
        kc_scr[pg * per_page:(pg + 1) * per_page, :] = jnp.mean(ck.reshape(per_page, CMP_BLOCK, kd), axis=1)
        vc_scr[pg * per_page:(pg + 1) * per_page, :] = jnp.mean(cv.reshape(per_page, CMP_BLOCK, kd), axis=1)
        ks_scr[pg * page:(pg + 1) * page, :] = sk_pages[pg][0, 0].astype(BF16)
        vs_scr[pg * page:(pg + 1) * page, :] = sv_pages[pg][0, 0].astype(BF16)
    kvs = kvs_ref[0]
    kvw = kvw_ref[0]
    ks_scr[past:past + SAMPLE_PAD, :] = pad_tile(kvs[:, :kd])
    vs_scr[past:past + SAMPLE_PAD, :] = pad_tile(kvs[:, kd:])
    wk = wk_ref[0, 0]
    wv = wv_ref[0, 0]
    kw_scr[0:wb, :] = wk.astype(BF16)
    vw_scr[0:wb, :] = wv.astype(BF16)
    kw_scr[wb:wb + SAMPLE_PAD, :] = pad_tile(kvw[:, :kd])
    vw_scr[wb:wb + SAMPLE_PAD, :] = pad_tile(kvw[:, kd:])
    nwk_ref[0, 0, 0:wb - t_new, :] = wk[t_new:, :]
    nwk_ref[0, 0, wb - t_new:wb, :] = kvw[:, :kd]
    nwv_ref[0, 0, 0:wb - t_new, :] = wv[t_new:, :]
    nwv_ref[0, 0, wb - t_new:wb, :] = kvw[:, kd:]

    q2 = q_ref[0]
    tok = _iota((nrow, 1), 0) % t_new

    p = _softmax_rows(_dot_nt(q2, kc_scr[...]) + bc_ref[...])
    o_c = _dot(p, vc_scr[...])
    ngt = NSA_GROUPS * t_new
    per_g = nrow // NSA_GROUPS
    ra = _iota((ngt, nrow), 0)
    rb = _iota((ngt, nrow), 1)
    same = (ra // t_new == rb // per_g) & (ra % t_new == rb % t_new)
    psum = _dot_hi(_ones_where(same, F32), p)
    ratio = SEL_BLOCK // CMP_BLOCK
    fold = _ones_where(_iota((ncb, LANES), 0) // ratio == _iota((ncb, LANES), 1), F32)
    imp = _dot_hi(psum, fold)
    jj = _iota((ngt, LANES), 1)
    cur = (past + _iota((ngt, LANES), 0) % t_new) // SEL_BLOCK
    forced = (jj == 0) | (jj == cur) | (jj == cur - 1)
    score = jnp.where(forced, FORCE_SCORE, jnp.where(jj <= cur, imp, NEG_INF))
    sel = _topk_mask(score, SEL_TOPK)
    ea = _iota((nrow, ngt), 0)
    eb = _iota((nrow, ngt), 1)
    spread = _ones_where((eb // t_new == ea // per_g) & (eb % t_new == ea % t_new), BF16)
    sel_rows = jnp.dot(spread, sel.astype(BF16), preferred_element_type=F32)
    chosen = jnp.dot(sel_rows.astype(BF16), ex_ref[...], preferred_element_type=F32)

    lk = past + SAMPLE_PAD
    kpos = _iota((nrow, lk), 1)
    keep = (chosen > 0.5) & (kpos <= past + tok)
    s = jnp.where(keep, _dot_nt(q2, ks_scr[...]) + bs_ref[...], NEG_INF)
    o_s = _dot(_softmax_rows(s), vs_scr[...])

    dist = wb + tok - _iota((nrow, wb + SAMPLE_PAD), 1)
    s = jnp.where((dist >= 0) & (dist < WINDOW), _dot_nt(q2, kw_scr[...]) + bw_ref[...], NEG_INF)
    o_w = _dot(_softmax_rows(s), vw_scr[...])

    gates = _sigmoid(gate_ref[0])
    o_ref[0] = gates[:, 0:1] * o_c + gates[:, 1:2] * o_s + gates[:, 2:3] * o_w


def nsa_sample(p_rows, nseq, rel_bias, caches, win_k, win_v, page_table):
    m = p_rows.shape[0]
    t = m // nseq
    d = NSA_HEADDIM
    g = NSA_GROUPS
    kd = g * d
    n_pages = page_table.shape[1]
    page = caches[0].shape[2]
    past = n_pages * page
    wb = win_k.shape[2]
    nrow = g * NSA_REP * t
    ncb = past // CMP_BLOCK
    lk = past + SAMPLE_PAD
    lw = wb + SAMPLE_PAD

    q = p_rows[:, COL_Q:COL_Q + NSA_HEADS * d].reshape(nseq, t, g, NSA_REP, d) * (d ** -0.5)
    q = q.transpose(0, 2, 3, 1, 4)
    eye_g = jnp.eye(g, dtype=F32)
    q2 = (q[:, :, :, :, None, :] * eye_g[None, :, None, None, :, None]).reshape(nseq, nrow, kd).astype(BF16)
    graw = p_rows[:, COL_MISC + SSD_HEADS:COL_MISC + SSD_HEADS + 3 * NSA_HEADS]
    graw = graw.reshape(nseq, t, 3, g, NSA_REP).transpose(0, 3, 4, 1, 2).reshape(nseq, nrow, 3)

    rel = rel_bias.astype(F32)
    tt = np.arange(t)
    head_rows = lambda tab: tab.transpose(2, 0, 1).reshape(g, NSA_REP, t, -1).reshape(nrow, -1)
    bias_c = head_rows(rel[_rel_bucket_np(past + tt[:, None] - (np.arange(ncb)[None, :] * CMP_BLOCK + CMP_BLOCK - 1))])
    bias_s = head_rows(rel[_rel_bucket_np(past + tt[:, None] - np.arange(lk)[None, :])])
    bias_w = head_rows(rel[_rel_bucket_np(wb + tt[:, None] - np.arange(lw)[None, :])])
    expand = jnp.asarray(np.arange(LANES)[:, None] == (np.arange(lk)[None, :] // SEL_BLOCK), BF16)

    p3 = p_rows.reshape(nseq, t, IN0_PAD)
    pt_flat = page_table.reshape(-1).astype(jnp.int32)
    c4 = [c.reshape(c.shape[0], c.shape[1], page, kd) for c in caches]
    wk4 = win_k.reshape(nseq, win_k.shape[1], wb, kd)
    wv4 = win_v.reshape(nseq, win_v.shape[1], wb, kd)
    const2 = lambda s, pt: (0, 0)
    in_specs = [pl.BlockSpec((1, nrow, kd), lambda s, pt: (s, 0, 0)),
                pl.BlockSpec((1, nrow, 3), lambda s, pt: (s, 0, 0)),
                pl.BlockSpec((1, t, 2 * kd), lambda s, pt: (s, 0, (COL_KV + 2 * kd) // (2 * kd))),
                pl.BlockSpec((1, t, 2 * kd), lambda s, pt: (s, 0, (COL_KV + 4 * kd) // (2 * kd))),
                pl.BlockSpec((1, 1, wb, kd), lambda s, pt: (s, 0, 0, 0)),
                pl.BlockSpec((1, 1, wb, kd), lambda s, pt: (s, 0, 0, 0)),
                pl.BlockSpec((nrow, ncb), const2), pl.BlockSpec((nrow, lk), const2),
                pl.BlockSpec((nrow, lw), const2), pl.BlockSpec((LANES, lk), const2)]
    args = [q2, graw, p3, p3, wk4, wv4, bias_c, bias_s, bias_w, expand]
    for c in c4:
        for pg in range(n_pages):
            in_specs.append(pl.BlockSpec((1, 1, page, kd),
                                         lambda s, pt, pg=pg: (pt[s * n_pages + pg], 0, 0, 0)))
            args.append(c)
    grid_spec = pltpu.PrefetchScalarGridSpec(
        num_scalar_prefetch=1,
        grid=(nseq,),
        in_specs=in_specs,
        out_specs=[pl.BlockSpec((1, nrow, kd), lambda s, pt: (s, 0, 0)),
                   pl.BlockSpec((1, 1, wb, kd), lambda s, pt: (s, 0, 0, 0)),
                   pl.BlockSpec((1, 1, wb, kd), lambda s, pt: (s, 0, 0, 0))],
        scratch_shapes=[pltpu.VMEM((ncb, kd), F32), pltpu.VMEM((ncb, kd), F32),
                        pltpu.VMEM((lk, kd), BF16), pltpu.VMEM((lk, kd), BF16),
                        pltpu.VMEM((lw, kd), BF16), pltpu.VMEM((lw, kd), BF16)])
    o, nwk, nwv = pl.pallas_call(
        functools.partial(_nsa_sample_kernel, n_pages=n_pages, t_new=t),
        grid_spec=grid_spec,
        out_shape=[jax.ShapeDtypeStruct((nseq, nrow, kd), F32),
                   jax.ShapeDtypeStruct((nseq, 1, wb, kd), F32),
                   jax.ShapeDtypeStruct((nseq, 1, wb, kd), F32)],
        compiler_params=_cparams(("arbitrary",)),
        name="nsa_sample",
    )(pt_flat, *args)
    o = o.reshape(nseq, g, NSA_REP, t, g, d)
    o = jnp.stack([o[:, gi, :, :, gi, :] for gi in range(g)], axis=1)
    o = o.transpose(0, 3, 1, 2, 4).reshape(m, g * NSA_REP * d)
    return o, nwk.reshape(nseq, 1, wb, g, d), nwv.reshape(nseq, 1, wb, g, d)


CONF_ROWS = 256


def _trunk(x, nseq, mem_k, mem_v, w, st):
    m, dmod = x.shape
    L = m // nseq
    p_rows = norm_matmul(x, w['norm_mix'][0], w['w_in0'])
    q_ssd = SSD_CHUNK if L % SSD_CHUNK == 0 else L
    p3 = p_rows.reshape(m // q_ssd, q_ssd, IN0_PAD)
    if st is None:
        y_ssd, h_pairs = ssd_mixer(p3, nseq, None, None, *w['ssd'])
        o_nsa = nsa_prompt(p_rows, nseq, w['rel_bias'])
        wbuf = min(WINDOW, L)
    else:
        y_ssd, h_pairs = ssd_mixer(p3, nseq, st['ssd_conv'], _state_to_pairs(st['ssm'][:, 0]), *w['ssd'])
        o_nsa, new_wk, new_wv = nsa_sample(p_rows, nseq, w['rel_bias'], st['pages'], st['win_k'], st['win_v'],
                                           st['page_table'])
    ssm = _pairs_to_state(h_pairs)[:, None]
    pr = p_rows.reshape(nseq, L, IN0_PAD)
    dconv = SSD_HEADS * SSD_HEADDIM + 2 * SSD_GROUPS * SSD_STATE
    sconv = pr[:, L - (SSD_CONV - 1):, COL_X:COL_X + dconv][:, None]
    kd = NSA_GROUPS * NSA_HEADDIM
    kv = [pr[:, :, COL_KV + i * kd:COL_KV + (i + 1) * kd].reshape(nseq, 1, L, NSA_GROUPS, NSA_HEADDIM)
          for i in range(6)]
    if st is None:
        new_wk, new_wv = kv[4][:, :, L - wbuf:], kv[5][:, :, L - wbuf:]
    x = matmul_res([y_ssd.reshape(m, -1), o_nsa], [w['w_out0_ssd'], w['w_out0_nsa']], x)
    x = cross_attend(x, nseq, w['norm_cross'][0], w['mem_wq'][0], w['mem_wo'][0], mem_k, mem_v, 0)
    x = swiglu_ffn(x, w['norm_ffn'][0], w['ffn_wg'], w['ffn_wu'], w['ffn_wd'])
    ag = norm_matmul(x, w['norm_mix'][1], w['conf_w_in'])
    tc = CONF_ROWS if L % CONF_ROWS == 0 else L
    xo, conf_buf = conf_mixer(ag.reshape(m // tc, tc, -1), x.reshape(m // tc, tc, dmod), nseq,
                              None if st is None else st['conf_conv'], *w['conf'])
    x = xo.reshape(m, dmod)
    x = cross_attend(x, nseq, w['norm_cross'][1], w['mem_wq'][1], w['mem_wo'][1], mem_k, mem_v, 1)
    y = moe_ffn_final(x, w['norm_ffn'][1], w['moe_router'], w['moe_wg'], w['moe_wu'], w['moe_wd'], w['norm_final'])
    return (y.reshape(nseq, L, dmod), ssm, sconv, kv[0], kv[1], kv[2], kv[3], new_wk, new_wv, conf_buf[:, None])


def kernel(x_prompt, x_sample, mem_prompt, cache_mem_k, cache_mem_v, cache_nsa_cmp_k, cache_nsa_cmp_v,
           cache_nsa_sel_k, cache_nsa_sel_v, cache_nsa_win_k, cache_nsa_win_v, state_ssm, state_ssd_conv,
           state_conf_conv, page_table, norm_mix, norm_cross, norm_ffn, norm_final, w_in0, w_out0, ssd_conv_w,
           ssd_conv_b, ssd_dt_bias, ssd_a_log, ssd_d, ssd_norm, rel_bias, conf_w_in, conf_dw, conf_dw_b,
           conf_ln_g, conf_ln_b, conf_w_out, mem_wq, mem_wk, mem_wv, mem_wo, ffn_wg, ffn_wu, ffn_wd,
           moe_router, moe_wg, moe_wu, moe_wd):
    bf = lambda a: a.astype(BF16)
    d_ssd = SSD_HEADS * SSD_HEADDIM
    w = {
        'norm_mix': norm_mix, 'norm_cross': norm_cross, 'norm_ffn': norm_ffn, 'norm_final': norm_final,
        'w_in0': _reorder_in0(w_in0[0]),
        'w_out0_ssd': bf(w_out0[0, :d_ssd]), 'w_out0_nsa': bf(w_out0[0, d_ssd:]),
        'ssd': (ssd_conv_w[0], ssd_conv_b[0], ssd_dt_bias[0], ssd_a_log[0], ssd_d[0], ssd_norm[0]),
        'rel_bias': rel_bias,
        'conf_w_in': bf(conf_w_in[0]),
        'conf': (conf_dw[0], conf_dw_b[0], conf_ln_g[0], conf_ln_b[0], bf(conf_w_out[0])),
        'mem_wq': bf(mem_wq), 'mem_wo': bf(mem_wo),
        'ffn_wg': bf(ffn_wg[0]), 'ffn_wu': bf(ffn_wu[0]), 'ffn_wd': bf(ffn_wd[0]),
        'moe_router': moe_router[0], 'moe_wg': bf(moe_wg[0]), 'moe_wu': bf(moe_wu[0]), 'moe_wd': bf(moe_wd[0]),
    }
    bp, sp, dmod = x_prompt.shape
    bs, ss, _ = x_sample.shape
    mk_p, mv_p = mem_kv(mem_prompt, bf(mem_wk), bf(mem_wv))
    mem_shape = (bp, mem_wk.shape[0], mem_prompt.shape[1], MEM_HEADS, MEM_HEADDIM)
    outs_p = _trunk(x_prompt.reshape(bp * sp, dmod), bp, mk_p, mv_p, w, None)
    st = {'ssm': state_ssm, 'ssd_conv': state_ssd_conv, 'win_k': cache_nsa_win_k, 'win_v': cache_nsa_win_v,
          'conf_conv': state_conf_conv, 'page_table': page_table,
          'pages': (cache_nsa_cmp_k, cache_nsa_cmp_v, cache_nsa_sel_k, cache_nsa_sel_v)}
    outs_s = _trunk(x_sample.reshape(bs * ss, dmod), bs, cache_mem_k, cache_mem_v, w, st)
    return (outs_p[0], outs_s[0], mk_p.reshape(mem_shape), mv_p.reshape(mem_shape)) + outs_p[1:] + outs_s[1:]
```

```python
import functools
import math

import numpy as np
import jax
import jax.numpy as jnp
from jax import lax
from jax.experimental import pallas as pl
from jax.experimental.pallas import tpu as pltpu

F32 = jnp.float32
BF16 = jnp.bfloat16

SSD_HEADS = 16
SSD_HEADDIM = 64
SSD_GROUPS = 2
SSD_STATE = 128
SSD_CONV = 4
SSD_CHUNK = 128
NSA_HEADS = 16
NSA_HEADDIM = 64
NSA_GROUPS = 2
NSA_REP = 8
CMP_BLOCK = 32
SEL_BLOCK = 64
SEL_TOPK = 16
WINDOW = 512
Q_BLOCK = 128
REL_BUCKETS = 32
REL_MAX_DIST = 1024
CONF_KERNEL = 31
MEM_HEADS = 4
MEM_HEADDIM = 128
N_EXPERTS = 8
RMS_EPS = 1e-6
LN_EPS = 1e-5
NEG_INF = -1e30
FORCE_SCORE = 1e30

LANES = 128
SUBLANES = 8
VMEM_LIMIT = 56 * 1024 * 1024

COL_Z = 0
COL_X = 1024
COL_BC = 2048
COL_Q = 2560
COL_KV = 3584
COL_MISC = 4352
IN0_PAD = 4608

HI = lax.Precision.HIGHEST


def _cparams(sem):
    return pltpu.CompilerParams(dimension_semantics=sem, vmem_limit_bytes=VMEM_LIMIT)


def _pick(n, pref):
    for t in pref:
        if n % t == 0:
            return t
    return n


def _silu(x):
    return x * (1.0 / (1.0 + jnp.exp(-x)))


def _sigmoid(x):
    return 1.0 / (1.0 + jnp.exp(-x))


def _dot(a, b):
    return jnp.dot(a.astype(BF16), b.astype(BF16), preferred_element_type=F32)


def _dot_nt(a, b):
    return lax.dot_general(a.astype(BF16), b.astype(BF16), (((1,), (1,)), ((), ())),
                           preferred_element_type=F32)


def _dot_tn(a, b):
    return lax.dot_general(a.astype(BF16), b.astype(BF16), (((0,), (0,)), ((), ())),
                           preferred_element_type=F32)


def _dot_hi(a, b):
    return jnp.dot(a, b, precision=HI, preferred_element_type=F32)


def _dot_nt_hi(a, b):
    return lax.dot_general(a, b, (((1,), (1,)), ((), ())), precision=HI, preferred_element_type=F32)


def _iota(shape, dim):
    return lax.broadcasted_iota(jnp.int32, shape, dim)


def _ones_where(cond, dtype):
    return jnp.where(cond, 1.0, 0.0).astype(dtype)


def _rms(x, g):
    return x * lax.rsqrt(jnp.mean(x * x, axis=-1, keepdims=True) + RMS_EPS) * g


def _norm_matmul_kernel(x_ref, g_ref, w_ref, o_ref, xn_ref):
    @pl.when(pl.program_id(1) == 0)
    def _():
        xn_ref[...] = _rms(x_ref[...], g_ref[...]).astype(BF16)

    o_ref[...] = jnp.dot(xn_ref[...], w_ref[...], preferred_element_type=F32)


def norm_matmul(x, g, w):
    m, k = x.shape
    n = w.shape[1]
    tm = _pick(m, (512, 256, 128, 64, 32, 16, 8))
    tn = _pick(n, (512, 256, 128))
    return pl.pallas_call(
        _norm_matmul_kernel,
        grid=(m // tm, n // tn),
        in_specs=[pl.BlockSpec((tm, k), lambda i, j: (i, 0)),
                  pl.BlockSpec((1, k), lambda i, j: (0, 0)),
                  pl.BlockSpec((k, tn), lambda i, j: (0, j))],
        out_specs=pl.BlockSpec((tm, tn), lambda i, j: (i, j)),
        out_shape=jax.ShapeDtypeStruct((m, n), F32),
        scratch_shapes=[pltpu.VMEM((tm, k), BF16)],
        compiler_params=_cparams(("parallel", "arbitrary")),
        name="norm_matmul",
    )(x, g.reshape(1, k), w)


def _matmul_res_kernel(*refs, n_in):
    a_refs = refs[:n_in]
    w_refs = refs[n_in:2 * n_in]
    r_ref, o_ref = refs[2 * n_in], refs[2 * n_in + 1]
    acc = r_ref[...]
    for a_ref, w_ref in zip(a_refs, w_refs):
        acc = acc + jnp.dot(a_ref[...].astype(BF16), w_ref[...], preferred_element_type=F32)
    o_ref[...] = acc


def matmul_res(a_list, w_list, r):
    m, n = r.shape
    tm = _pick(m, (512, 256, 128, 64, 32, 16, 8))
    tn = _pick(n, (512, 256, 128))
    n_in = len(a_list)
    in_specs = ([pl.BlockSpec((tm, a.shape[1]), lambda i, j: (i, 0)) for a in a_list]
                + [pl.BlockSpec((w.shape[0], tn), lambda i, j: (0, j)) for w in w_list]
                + [pl.BlockSpec((tm, tn), lambda i, j: (i, j))])
    return pl.pallas_call(
        functools.partial(_matmul_res_kernel, n_in=n_in),
        grid=(m // tm, n // tn),
        in_specs=in_specs,
        out_specs=pl.BlockSpec((tm, tn), lambda i, j: (i, j)),
        out_shape=jax.ShapeDtypeStruct((m, n), F32),
        compiler_params=_cparams(("parallel", "arbitrary")),
        name="matmul_res",
    )(*a_list, *w_list, r)


def _swiglu_kernel(x_ref, g_ref, wg_ref, wu_ref, wd_ref, o_ref, xn_ref, acc_ref):
    f = pl.program_id(1)

    @pl.when(f == 0)
    def _():
        xn_ref[...] = _rms(x_ref[...], g_ref[...]).astype(BF16)
        acc_ref[...] = jnp.zeros_like(acc_ref)

    xn = xn_ref[...]
    hg = jnp.dot(xn, wg_ref[...], preferred_element_type=F32)
    hu = jnp.dot(xn, wu_ref[...], preferred_element_type=F32)
    acc_ref[...] += jnp.dot((_silu(hg) * hu).astype(BF16), wd_ref[...], preferred_element_type=F32)

    @pl.when(f == pl.num_programs(1) - 1)
    def _():
        o_ref[...] = x_ref[...] + acc_ref[...]


def swiglu_ffn(x, g, wg, wu, wd):
    m, d = x.shape
    ff = wg.shape[1]
    tm = _pick(m, (512, 256, 128, 64, 32, 16, 8))
    tf = _pick(ff, (512, 256, 128))
    return pl.pallas_call(
        _swiglu_kernel,
        grid=(m // tm, ff // tf),
        in_specs=[pl.BlockSpec((tm, d), lambda i, f: (i, 0)),
                  pl.BlockSpec((1, d), lambda i, f: (0, 0)),
                  pl.BlockSpec((d, tf), lambda i, f: (0, f)),
                  pl.BlockSpec((d, tf), lambda i, f: (0, f)),
                  pl.BlockSpec((tf, d), lambda i, f: (f, 0))],
        out_specs=pl.BlockSpec((tm, d), lambda i, f: (i, 0)),
        out_shape=jax.ShapeDtypeStruct((m, d), F32),
        scratch_shapes=[pltpu.VMEM((tm, d), BF16), pltpu.VMEM((tm, d), F32)],
        compiler_params=_cparams(("parallel", "arbitrary")),
        name="swiglu_ffn",
    )(x, g.reshape(1, d), wg, wu, wd)


def _top2_gate(logits):
    lane = _iota(logits.shape, 1)
    m1 = jnp.max(logits, axis=-1, keepdims=True)
    i1 = jnp.min(jnp.where(logits == m1, lane, LANES), axis=-1, keepdims=True)
    rest = jnp.where(lane == i1, -jnp.inf, logits)
    m2 = jnp.max(rest, axis=-1, keepdims=True)
    i2 = jnp.min(jnp.where(rest == m2, lane, LANES), axis=-1, keepdims=True)
    e2 = jnp.exp(m2 - m1)
    den = 1.0 + e2
    return jnp.where(lane == i1, 1.0 / den, 0.0) + jnp.where(lane == i2, e2 / den, 0.0)


def _moe_kernel(x_ref, g_ref, wr_ref, wg_ref, wu_ref, wd_ref, gf_ref, o_ref, xn_ref, gate_ref, acc_ref):
    e = pl.program_id(1)
    f = pl.program_id(2)

    @pl.when((e == 0) & (f == 0))
    def _():
        xn = _rms(x_ref[...], g_ref[...]).astype(BF16)
        xn_ref[...] = xn
        logits = jnp.dot(xn, wr_ref[...], preferred_element_type=F32)
        lane = _iota(logits.shape, 1)
        gate_ref[...] = _top2_gate(jnp.where(lane < N_EXPERTS, logits, -jnp.inf))
        acc_ref[...] = jnp.zeros_like(acc_ref)

    xn = xn_ref[...]
    gate = gate_ref[...]
    lane = _iota(gate.shape, 1)
    ge = jnp.sum(jnp.where(lane == e, gate, 0.0), axis=-1, keepdims=True)
    hg = jnp.dot(xn, wg_ref[0], preferred_element_type=F32)
    hu = jnp.dot(xn, wu_ref[0], preferred_element_type=F32)
    y = jnp.dot((_silu(hg) * hu).astype(BF16), wd_ref[0], preferred_element_type=F32)
    acc_ref[...] += ge * y

    @pl.when((e == pl.num_programs(1) - 1) & (f == pl.num_programs(2) - 1))
    def _():
        o_ref[...] = _rms(x_ref[...] + acc_ref[...], gf_ref[...])


def moe_ffn_final(x, g, w_router, wg, wu, wd, g_final):
    m, d = x.shape
    ne, _, ff = wg.shape
    tm = _pick(m, (512, 256, 128, 64, 32, 16, 8))
    tf = _pick(ff, (512, 256, 128))
    wr = jnp.zeros((d, LANES), BF16).at[:, :ne].set(w_router.astype(BF16))
    return pl.pallas_call(
        _moe_kernel,
        grid=(m // tm, ne, ff // tf),
        in_specs=[pl.BlockSpec((tm, d), lambda i, e, f: (i, 0)),
                  pl.BlockSpec((1, d), lambda i, e, f: (0, 0)),
                  pl.BlockSpec((d, LANES), lambda i, e, f: (0, 0)),
                  pl.BlockSpec((1, d, tf), lambda i, e, f: (e, 0, f)),
                  pl.BlockSpec((1, d, tf), lambda i, e, f: (e, 0, f)),
                  pl.BlockSpec((1, tf, d), lambda i, e, f: (e, f, 0)),
                  pl.BlockSpec((1, d), lambda i, e, f: (0, 0))],
        out_specs=pl.BlockSpec((tm, d), lambda i, e, f: (i, 0)),
        out_shape=jax.ShapeDtypeStruct((m, d), F32),
        scratch_shapes=[pltpu.VMEM((tm, d), BF16), pltpu.VMEM((tm, LANES), F32), pltpu.VMEM((tm, d), F32)],
        compiler_params=_cparams(("parallel", "arbitrary", "arbitrary")),
        name="moe_ffn",
    )(x, g.reshape(1, d), wr, wg, wu, wd, g_final.reshape(1, d))


def _mem_kv_kernel(x_ref, wk_ref, wv_ref, k_ref, v_ref):
    x = x_ref[0].astype(BF16)
    k_ref[0, 0] = jnp.dot(x, wk_ref[0], preferred_element_type=F32)
    v_ref[0, 0] = jnp.dot(x, wv_ref[0], preferred_element_type=F32)


def mem_kv(mem, wk, wv):
    b, t, d = mem.shape
    nl, _, e = wk.shape
    out = jax.ShapeDtypeStruct((b, nl, t, e), F32)
    return pl.pallas_call(
        _mem_kv_kernel,
        grid=(b, nl),
        in_specs=[pl.BlockSpec((1, t, d), lambda i, l: (i, 0, 0)),
                  pl.BlockSpec((1, d, e), lambda i, l: (l, 0, 0)),
                  pl.BlockSpec((1, d, e), lambda i, l: (l, 0, 0))],
        out_specs=[pl.BlockSpec((1, 1, t, e), lambda i, l: (i, l, 0, 0)),
                   pl.BlockSpec((1, 1, t, e), lambda i, l: (i, l, 0, 0))],
        out_shape=[out, out],
        compiler_params=_cparams(("parallel", "arbitrary")),
        name="mem_kv",
    )(mem, wk, wv)


def _cross_attn_kernel(q_ref, k_ref, v_ref, o_ref):
    q = q_ref[0] * (MEM_HEADDIM ** -0.5)
    k = k_ref[0, 0]
    v = v_ref[0, 0]
    outs = []
    for h in range(MEM_HEADS):
        sl = slice(h * MEM_HEADDIM, (h + 1) * MEM_HEADDIM)
        s = _dot_nt(q[:, sl], k[:, sl])
        p = jnp.exp(s - jnp.max(s, axis=-1, keepdims=True))
        p = p / jnp.sum(p, axis=-1, keepdims=True)
        outs.append(_dot(p, v[:, sl]))
    o_ref[0] = jnp.concatenate(outs, axis=-1)


def cross_attn_core(q, mk, mv, layer):
    s, t, e = q.shape
    mt = mk.shape[2]
    tq = _pick(t, (512, 256, 128)) if t >= 128 else t
    return pl.pallas_call(
        _cross_attn_kernel,
        grid=(s, t // tq),
        in_specs=[pl.BlockSpec((1, tq, e), lambda i, j: (i, j, 0)),
                  pl.BlockSpec((1, 1, mt, e), lambda i, j: (i, layer, 0, 0)),
                  pl.BlockSpec((1, 1, mt, e), lambda i, j: (i, layer, 0, 0))],
        out_specs=pl.BlockSpec((1, tq, e), lambda i, j: (i, j, 0)),
        out_shape=jax.ShapeDtypeStruct((s, t, e), F32),
        compiler_params=_cparams(("parallel", "arbitrary")),
        name="cross_attn",
    )(q, mk, mv)


def cross_attend(x, nseq, g, wq, wo, mk, mv, layer):
    m, d = x.shape
    e = wq.shape[1]
    q = norm_matmul(x, g, wq)
    mk = mk.reshape(mk.shape[0], mk.shape[1], mk.shape[2], e)
    mv = mv.reshape(mk.shape)
    o = cross_attn_core(q.reshape(nseq, m // nseq, e), mk, mv, layer)
    return matmul_res([o.reshape(m, e)], [wo], x)


def _ssd_kernel(*refs, lreal, has_state):
    if has_state:
        (z_ref, x_ref, bc_ref, dt_ref, hist_ref, h0_ref, cw_ref, cb_ref, dtb_ref, alog_ref, d_ref, nw_ref,
         y_ref, hout_ref, xp_scr, st_scr, y_scr) = refs
    else:
        (z_ref, x_ref, bc_ref, dt_ref, cw_ref, cb_ref, dtb_ref, alog_ref, d_ref, nw_ref,
         y_ref, hout_ref, xp_scr, st_scr, y_scr) = refs
    c = pl.program_id(1)
    q8 = xp_scr.shape[0] - SUBLANES
    dx = x_ref.shape[-1]
    n = SSD_STATE
    hist = SSD_CONV - 1

    @pl.when(c == 0)
    def _():
        xp_scr[...] = jnp.zeros_like(xp_scr)
        if has_state:
            xp_scr[SUBLANES - hist:SUBLANES, :] = hist_ref[0, 0]
            st_scr[...] = h0_ref[0]
        else:
            st_scr[...] = jnp.zeros_like(st_scr)

    xp_scr[SUBLANES:SUBLANES + lreal, :dx] = x_ref[0]
    xp_scr[SUBLANES:SUBLANES + lreal, dx:] = bc_ref[0]
    conv = cb_ref[...]
    for k in range(SSD_CONV):
        lo = SUBLANES - hist + k
        conv = conv + cw_ref[k:k + 1, :] * xp_scr[lo:lo + q8, :]
    xc = _silu(conv)
    if q8 == lreal:
        xp_scr[0:SUBLANES, :] = xp_scr[q8:q8 + SUBLANES, :]

    lane = _iota((q8, LANES), 1)
    row = _iota((q8, LANES), 0)
    dt = jax.nn.softplus(dt_ref[0] + dtb_ref[...]) if q8 == lreal else None
    if dt is None:
        dtr = jnp.concatenate([dt_ref[0], jnp.zeros((q8 - lreal, LANES), F32)], axis=0)
        dt = jax.nn.softplus(dtr + dtb_ref[...])
    dt = jnp.where((lane < SSD_HEADS) & (row < lreal), dt, 0.0)
    ad = dt * (-jnp.exp(alog_ref[...]))
    tri = (_iota((q8, q8), 0) >= _iota((q8, q8), 1))
    acum = _dot_hi(tri.astype(F32), ad)
    eye = (_iota((LANES, LANES), 0) == _iota((LANES, LANES), 1)).astype(F32)
    acum_t = _dot_nt_hi(eye, acum)
    alast = acum[q8 - 1:q8, :]
    lo_half = lane < SSD_HEADDIM

    for g in range(SSD_GROUPS):
        bg = xc[:, dx + g * n:dx + (g + 1) * n]
        cg = xc[:, dx + (SSD_GROUPS + g) * n:dx + (SSD_GROUPS + g + 1) * n]
        cb = _dot_nt(cg, bg)
        hpg = SSD_HEADS // SSD_GROUPS
        for jp in range(hpg // 2):
            pair = g * (hpg // 2) + jp
            ha, hb = 2 * pair, 2 * pair + 1

            def sel(col_a, col_b):
                return jnp.where(lo_half, col_a, col_b)

            xs = xc[:, pair * LANES:(pair + 1) * LANES]
            xd = xs * sel(dt[:, ha:ha + 1], dt[:, hb:hb + 1])
            ys = []
            for h in (ha, hb):
                diff = acum[:, h:h + 1] - acum_t[h:h + 1, :]
                lmat = jnp.exp(jnp.where(tri, diff, NEG_INF))
                ys.append(_dot(cb * lmat, xd))
            y_diag = jnp.where(lo_half, ys[0], ys[1])
            st = st_scr[pair]
            y_off = _dot(cg, st) * sel(jnp.exp(acum[:, ha:ha + 1]), jnp.exp(acum[:, hb:hb + 1]))
            ds = sel(jnp.exp(alast[:, ha:ha + 1] - acum[:, ha:ha + 1]),
                     jnp.exp(alast[:, hb:hb + 1] - acum[:, hb:hb + 1]))
            dec = jnp.where(lo_half[0:1], jnp.exp(alast[:, ha:ha + 1]), jnp.exp(alast[:, hb:hb + 1]))
            st_scr[pair] = dec * st + _dot_tn(bg, xd * ds)
            dsk = jnp.where(lo_half[0:1], d_ref[:, ha:ha + 1], d_ref[:, hb:hb + 1])
            y_scr[:, pair * LANES:(pair + 1) * LANES] = y_diag + y_off + xs * dsk

    y = y_scr[...]
    if q8 != lreal:
        z = jnp.concatenate([z_ref[0], jnp.zeros((q8 - lreal, dx), F32)], axis=0)
    else:
        z = z_ref[0]
    y = y * _silu(z)
    half = dx // SSD_GROUPS
    lane_x = _iota((q8, dx), 1)
    ms0 = jnp.mean(jnp.square(y[:, :half]), axis=-1, keepdims=True)
    ms1 = jnp.mean(jnp.square(y[:, half:]), axis=-1, keepdims=True)
    scale = jnp.where(lane_x < half, lax.rsqrt(ms0 + RMS_EPS), lax.rsqrt(ms1 + RMS_EPS))
    y_ref[0] = (y * scale * nw_ref[...])[:lreal]

    @pl.when(c == pl.num_programs(1) - 1)
    def _():
        hout_ref[0] = st_scr[...]


def _pad_lanes(v):
    return jnp.zeros((1, LANES), F32).at[0, :v.shape[0]].set(v.astype(F32))


def ssd_mixer(p3, nseq, conv_hist, h0, cw, cb, dt_bias, a_log, d_skip, norm_w):
    nblk, q, _ = p3.shape
    nc = nblk // nseq
    dx = SSD_HEADS * SSD_HEADDIM
    dbc = 2 * SSD_GROUPS * SSD_STATE
    q8 = -(-q // SUBLANES) * SUBLANES
    npair = SSD_HEADS // 2
    has_state = h0 is not None
    row = lambda s, c: (s * nc + c, 0, 0)
    in_specs = [pl.BlockSpec((1, q, dx), lambda s, c: (s * nc + c, 0, COL_Z // dx)),
                pl.BlockSpec((1, q, dx), lambda s, c: (s * nc + c, 0, COL_X // dx)),
                pl.BlockSpec((1, q, dbc), lambda s, c: (s * nc + c, 0, COL_BC // dbc)),
                pl.BlockSpec((1, q, LANES), lambda s, c: (s * nc + c, 0, COL_MISC // LANES))]
    args = [p3, p3, p3, p3]
    if has_state:
        in_specs += [pl.BlockSpec((1, 1, SSD_CONV - 1, dx + dbc), lambda s, c: (s, 0, 0, 0)),
                     pl.BlockSpec((1, npair, SSD_STATE, LANES), lambda s, c: (s, 0, 0, 0))]
        args += [conv_hist, h0]
    const = lambda s, c: (0, 0)
    in_specs += [pl.BlockSpec((SSD_CONV, dx + dbc), const), pl.BlockSpec((1, dx + dbc), const),
                 pl.BlockSpec((1, LANES), const), pl.BlockSpec((1, LANES), const),
                 pl.BlockSpec((1, LANES), const), pl.BlockSpec((1, dx), const)]
    args += [cw, cb.reshape(1, -1), _pad_lanes(dt_bias), _pad_lanes(a_log), _pad_lanes(d_skip),
             norm_w.reshape(1, dx)]
    del row
    return pl.pallas_call(
        functools.partial(_ssd_kernel, lreal=q, has_state=has_state),
        grid=(nseq, nc),
        in_specs=in_specs,
        out_specs=[pl.BlockSpec((1, q, dx), lambda s, c: (s * nc + c, 0, 0)),
                   pl.BlockSpec((1, npair, SSD_STATE, LANES), lambda s, c: (s, 0, 0, 0))],
        out_shape=[jax.ShapeDtypeStruct((nblk, q, dx), F32),
                   jax.ShapeDtypeStruct((nseq, npair, SSD_STATE, LANES), F32)],
        scratch_shapes=[pltpu.VMEM((SUBLANES + q8, dx + dbc), F32),
                        pltpu.VMEM((npair, SSD_STATE, LANES), F32),
                        pltpu.VMEM((q8, dx), F32)],
        compiler_params=_cparams(("parallel", "arbitrary")),
        name="ssd_mixer",
    )(*args)


def _state_to_pairs(h):
    s, nh, p, n = h.shape
    return h.reshape(s, nh // 2, 2, p, n).transpose(0, 1, 4, 2, 3).reshape(s, nh // 2, n, 2 * p)


def _pairs_to_state(hp):
    s, npair, n, pp = hp.shape
    return hp.reshape(s, npair, n, 2, pp // 2).transpose(0, 1, 3, 4, 2).reshape(s, 2 * npair, pp // 2, n)


def _reorder_in0(w):
    d = w.shape[0]
    dz = SSD_HEADS * SSD_HEADDIM
    dxbc = dz + 2 * SSD_GROUPS * SSD_STATE
    o_dt = dz + dxbc
    o_q = o_dt + SSD_HEADS
    o_kv = o_q + NSA_HEADS * NSA_HEADDIM
    o_g = o_kv + 6 * NSA_GROUPS * NSA_HEADDIM
    parts = [w[:, :o_dt], w[:, o_q:o_g], w[:, o_dt:o_q], w[:, o_g:]]
    wn = jnp.concatenate(parts, axis=1)
    return jnp.pad(wn, ((0, 0), (0, IN0_PAD - wn.shape[1]))).astype(BF16)


CONF_HALO = 32


def _conf_kernel(*refs, lreal, has_state):
    if has_state:
        ag_ref, x_ref, hist_ref, dw_ref, dwb_ref, lg_ref, lb_ref, wo_ref, o_ref, buf_ref, up_scr = refs
    else:
        ag_ref, x_ref, dw_ref, dwb_ref, lg_ref, lb_ref, wo_ref, o_ref, buf_ref, up_scr = refs
    c = pl.program_id(1)
    ch = x_ref.shape[-1]
    t8 = up_scr.shape[0] - CONF_HALO
    nh = CONF_KERNEL - 1

    @pl.when(c == 0)
    def _():
        up_scr[...] = jnp.zeros_like(up_scr)
        if has_state:
            up_scr[CONF_HALO - nh:CONF_HALO, :] = hist_ref[0, 0]

    ag = ag_ref[0]
    up_scr[CONF_HALO:CONF_HALO + lreal, :] = ag[:, :ch] * _sigmoid(ag[:, ch:])
    acc = dwb_ref[...] + dw_ref[0:1, :] * up_scr[CONF_HALO - nh:CONF_HALO - nh + t8, :]
    for k in range(1, CONF_KERNEL):
        lo = CONF_HALO - nh + k
        acc = acc + dw_ref[k:k + 1, :] * up_scr[lo:lo + t8, :]
    mu = jnp.mean(acc, axis=-1, keepdims=True)
    cen = acc - mu
    var = jnp.mean(cen * cen, axis=-1, keepdims=True)
    y = _silu(cen * lax.rsqrt(var + LN_EPS) * lg_ref[...] + lb_ref[...])
    o_ref[0] = x_ref[0] + jnp.dot(y.astype(BF16), wo_ref[...], preferred_element_type=F32)[:lreal]

    @pl.when(c == pl.num_programs(1) - 1)
    def _():
        buf_ref[0] = up_scr[CONF_HALO + lreal - nh:CONF_HALO + lreal, :]

    if t8 == lreal:
        up_scr[0:CONF_HALO, :] = up_scr[t8:t8 + CONF_HALO, :]


def conf_mixer(ag3, x3, nseq, hist, dw, dwb, ln_g, ln_b, w_out):
    nblk, t, ch = x3.shape
    nc = nblk // nseq
    t8 = -(-t // SUBLANES) * SUBLANES
    nh = CONF_KERNEL - 1
    has_state = hist is not None
    const = lambda s, c: (0, 0)
    in_specs = [pl.BlockSpec((1, t, 2 * ch), lambda s, c: (s * nc + c, 0, 0)),
                pl.BlockSpec((1, t, ch), lambda s, c: (s * nc + c, 0, 0))]
    args = [ag3, x3]
    if has_state:
        in_specs.append(pl.BlockSpec((1, 1, nh, ch), lambda s, c: (s, 0, 0, 0)))
        args.append(hist)
    in_specs += [pl.BlockSpec((CONF_KERNEL, ch), const), pl.BlockSpec((1, ch), const),
                 pl.BlockSpec((1, ch), const), pl.BlockSpec((1, ch), const), pl.BlockSpec((ch, ch), const)]
    args += [dw, dwb.reshape(1, ch), ln_g.reshape(1, ch), ln_b.reshape(1, ch), w_out]
    return pl.pallas_call(
        functools.partial(_conf_kernel, lreal=t, has_state=has_state),
        grid=(nseq, nc),
        in_specs=in_specs,
        out_specs=[pl.BlockSpec((1, t, ch), lambda s, c: (s * nc + c, 0, 0)),
                   pl.BlockSpec((1, nh, ch), lambda s, c: (s, 0, 0))],
        out_shape=[jax.ShapeDtypeStruct((nblk, t, ch), F32), jax.ShapeDtypeStruct((nseq, nh, ch), F32)],
        scratch_shapes=[pltpu.VMEM((CONF_HALO + t8, ch), F32)],
        compiler_params=_cparams(("parallel", "arbitrary")),
        name="conf_mixer",
    )(*args)


def _rel_bucket_np(dist):
    n = np.maximum(dist, 0)
    max_exact = REL_BUCKETS // 2
    nf = np.maximum(n, max_exact).astype(np.float32)
    ratio = np.log(nf / np.float32(max_exact)) / np.float32(math.log(REL_MAX_DIST / max_exact))
    large = max_exact + (ratio * np.float32(REL_BUCKETS - max_exact)).astype(np.int32)
    return np.where(n < max_exact, n, np.minimum(large, REL_BUCKETS - 1)).astype(np.int32)


_FAR_DIST = int(np.argmax(_rel_bucket_np(np.arange(4 * REL_MAX_DIST)) == REL_BUCKETS - 1))
TOEP_TILES = -(-(_FAR_DIST + Q_BLOCK - 1) // Q_BLOCK) + 1
CMP_NEAR = 32
assert CMP_BLOCK * (CMP_NEAR - 3) - (Q_BLOCK - 1) >= _FAR_DIST


def _split3(x):
    hi = x.astype(BF16)
    r1 = x - hi.astype(F32)
    mid = r1.astype(BF16)
    lo = (r1 - mid.astype(F32)).astype(BF16)
    return hi, mid, lo


def _topk_mask(score, k):
    nl = score.shape[1]
    lane = _iota(score.shape, 1)
    sel = jnp.zeros(score.shape, F32)
    for _ in range(k):
        m = jnp.max(score, axis=-1, keepdims=True)
        idx = jnp.min(jnp.where(score == m, lane, nl), axis=-1, keepdims=True)
        pick = lane == idx
        sel = jnp.where(pick, 1.0, sel)
        score = jnp.where(pick, -jnp.inf, score)
    return sel


def _softmax_rows(s):
    p = jnp.exp(s - jnp.max(s, axis=-1, keepdims=True))
    return p * (1.0 / jnp.sum(p, axis=-1, keepdims=True))


PREP_ROWS = 512


def _nsa_prep_kernel(q0_ref, q1_ref, kvc_ref, kvs_ref, kvw_ref, qr_ref, kcm_ref, vcm_ref, ks_ref, vs_ref,
                     kw_ref, vw_ref, rows_ref):
    d = NSA_HEADDIM
    for i, src in enumerate((kvc_ref, kvs_ref, kvw_ref)):
        rows_ref[2 * i, 0] = src[0][:, :NSA_GROUPS * d]
        rows_ref[2 * i + 1, 0] = src[0][:, NSA_GROUPS * d:]
    scale = d ** -0.5
    for g, q_ref in enumerate((q0_ref, q1_ref)):
        q = q_ref[0]
        for r in range(NSA_REP):
            qr_ref[0, g, r] = (q[:, r * d:(r + 1) * d] * scale).astype(BF16)
    kvc = kvc_ref[0]
    t = kvc.shape[0]
    means = jnp.mean(kvc.reshape(t // CMP_BLOCK, CMP_BLOCK, kvc.shape[1]), axis=1)
    kvs = kvs_ref[0]
    kvw = kvw_ref[0]
    for g in range(NSA_GROUPS):
        kcm_ref[0, g] = means[:, g * d:(g + 1) * d].astype(BF16)
        vcm_ref[0, g] = means[:, (NSA_GROUPS + g) * d:(NSA_GROUPS + g + 1) * d].astype(BF16)
        ks_ref[0, g] = kvs[:, g * d:(g + 1) * d].astype(BF16)
        vs_ref[0, g] = kvs[:, (NSA_GROUPS + g) * d:(NSA_GROUPS + g + 1) * d].astype(BF16)
        kw_ref[0, g] = kvw[:, g * d:(g + 1) * d].astype(BF16)
        vw_ref[0, g] = kvw[:, (NSA_GROUPS + g) * d:(NSA_GROUPS + g + 1) * d].astype(BF16)


def nsa_prep(p3, nseq):
    nblk, t, _ = p3.shape
    nt = nblk // nseq
    L = nt * t
    d = NSA_HEADDIM
    g = NSA_GROUPS
    qw = NSA_REP * d
    kvw = 2 * g * d
    blk = lambda width, col: pl.BlockSpec((1, t, width), lambda s, c: (s * nt + c, 0, col // width))
    seq4 = lambda rows: pl.BlockSpec((1, g, rows, d), lambda s, c: (s, 0, c, 0))
    return pl.pallas_call(
        _nsa_prep_kernel,
        grid=(nseq, nt),
        in_specs=[blk(qw, COL_Q), blk(qw, COL_Q + qw), blk(kvw, COL_KV), blk(kvw, COL_KV + kvw),
                  blk(kvw, COL_KV + 2 * kvw)],
        out_specs=[pl.BlockSpec((1, g, NSA_REP, t, d), lambda s, c: (s, 0, 0, c, 0)),
                   seq4(t // CMP_BLOCK), seq4(t // CMP_BLOCK), seq4(t), seq4(t), seq4(t), seq4(t),
                   pl.BlockSpec((6, 1, t, g * d), lambda s, c: (0, s * nt + c, 0, 0))],
        out_shape=[jax.ShapeDtypeStruct((nseq, g, NSA_REP, L, d), BF16),
                   jax.ShapeDtypeStruct((nseq, g, L // CMP_BLOCK, d), BF16),
                   jax.ShapeDtypeStruct((nseq, g, L // CMP_BLOCK, d), BF16)]
                  + [jax.ShapeDtypeStruct((nseq, g, L, d), BF16)] * 4
                  + [jax.ShapeDtypeStruct((6, nblk, t, g * d), F32)],
        compiler_params=_cparams(("parallel", "arbitrary")),
        name="nsa_prep",
    )(p3, p3, p3, p3, p3)


SEL_KEYS = 512
WIN_KEYS = WINDOW + Q_BLOCK


def _nsa_prompt_kernel(qr_ref, kcm_ref, vcm_ref, ks_ref, vs_ref, kw_ref, vw_ref, misc_ref, t4_ref, toep_ref,
                       o_ref, m_scr, l_scr, acc_scr):
    g = pl.program_id(1)
    qi = pl.program_id(2)
    tq = Q_BLOCK
    rows = NSA_REP * tq
    d = NSA_HEADDIM
    ncb = kcm_ref.shape[2]
    nsb = ncb * CMP_BLOCK // SEL_BLOCK
    L = ks_ref.shape[2]
    q2 = qr_ref[0, 0].reshape(rows, d)
    tok = _iota((rows, 1), 0) % tq
    qpos = qi * tq + tok

    s = _dot_nt(q2, kcm_ref[0, 0])
    kk = t4_ref.shape[2]
    e_row = _iota((kk, ncb), 0)
    c_col = _iota((kk, ncb), 1)
    near = 3 * CMP_NEAR
    shift = (e_row % CMP_NEAR) + c_col == qi * (tq // CMP_BLOCK) + 3
    p4 = _ones_where(((e_row < near) & shift) | ((e_row >= near) & (e_row < near + 3)), BF16)
    s = s + jnp.dot(t4_ref[0], p4, preferred_element_type=F32)
    cmp_end = _iota((rows, ncb), 1) * CMP_BLOCK + (CMP_BLOCK - 1)
    valid_c = qpos >= cmp_end
    s = jnp.where(valid_c, s, NEG_INF)
    p = _softmax_rows(s)
    p = jnp.where(valid_c, p, 0.0)
    o_c = _dot(p, vcm_ref[0, 0])
    psum = jnp.sum(p.reshape(NSA_REP, tq, ncb), axis=0)
    ratio = SEL_BLOCK // CMP_BLOCK
    fold = (_iota((ncb, nsb), 0) // ratio == _iota((ncb, nsb), 1)).astype(F32)
    imp = _dot_hi(psum, fold)

    jj = _iota((tq, nsb), 1)
    cur = (qi * tq + _iota((tq, nsb), 0)) // SEL_BLOCK
    forced = (jj == 0) | (jj == cur) | (jj == cur - 1)
    score = jnp.where(forced, FORCE_SCORE, jnp.where(jj <= cur, imp, NEG_INF))
    sel = _topk_mask(score, min(SEL_TOPK, nsb)).astype(BF16)

    m_scr[...] = jnp.full(m_scr.shape, NEG_INF, F32)
    l_scr[...] = jnp.zeros(l_scr.shape, F32)
    acc_scr[...] = jnp.zeros(acc_scr.shape, F32)
    tk = min(SEL_KEYS, L)
    sub = tk // tq
    qpos_t = qi * tq + _iota((tq, tk), 0)

    def sel_step(kt, carry):
        k0 = pl.multiple_of(kt * tk, tk)
        sc = _dot_nt(q2, ks_ref[0, 0, pl.ds(k0, tk), :])
        parts = []
        for u in range(sub):
            delta = jnp.clip(qi - (kt * sub + u), 0, TOEP_TILES - 1)
            parts.append(sc[:, u * tq:(u + 1) * tq] + toep_ref[0, delta])
        sc = jnp.concatenate(parts, axis=1) if sub > 1 else parts[0]
        kpos = k0 + _iota((tq, tk), 1)
        expand = _ones_where(_iota((nsb, tk), 0) == (k0 + _iota((nsb, tk), 1)) // SEL_BLOCK, BF16)
        chosen = jnp.dot(sel, expand, preferred_element_type=F32)
        keep = (chosen > 0.5) & (kpos <= qpos_t)
        sc = jnp.where(keep[None], sc.reshape(NSA_REP, tq, tk), NEG_INF).reshape(rows, tk)
        m_old = m_scr[...]
        m_new = jnp.maximum(m_old, jnp.max(sc, axis=-1, keepdims=True))
        alpha = jnp.exp(m_old - m_new)
        pexp = jnp.exp(sc - m_new)
        l_scr[...] = alpha * l_scr[...] + jnp.sum(pexp, axis=-1, keepdims=True)
        acc_scr[...] = alpha * acc_scr[...] + _dot(pexp, vs_ref[0, 0, pl.ds(k0, tk), :])
        m_scr[...] = m_new
        return carry

    lax.fori_loop(0, (qi * tq) // tk + 1, sel_step, 0)
    o_s = acc_scr[...] * (1.0 / l_scr[...])

    wk = min(WIN_KEYS, L)
    blk0 = jnp.maximum(qi - (wk // tq - 1), 0)
    w0 = pl.multiple_of(blk0 * tq, tq)
    kwin = kw_ref[0, 0, pl.ds(w0, wk), :]
    vwin = vw_ref[0, 0, pl.ds(w0, wk), :]
    sw = _dot_nt(q2, kwin)
    parts = []
    for u in range(wk // tq):
        delta = jnp.clip(qi - (blk0 + u), 0, TOEP_TILES - 1)
        parts.append(sw[:, u * tq:(u + 1) * tq] + toep_ref[0, delta])
    sw = jnp.concatenate(parts, axis=1)
    dist = qi * tq + _iota((tq, wk), 0) - (w0 + _iota((tq, wk), 1))
    in_win = (dist >= 0) & (dist < WINDOW)
    sw = jnp.where(in_win[None], sw.reshape(NSA_REP, tq, wk), NEG_INF).reshape(rows, wk)
    o_w = _dot(_softmax_rows(sw), vwin)

    gates = _sigmoid(misc_ref[0])
    lane = _iota((tq, LANES), 1)

    def gate_col(b, r):
        idx = SSD_HEADS + b * NSA_HEADS + g * NSA_REP + r
        return jnp.sum(jnp.where(lane == idx, gates, 0.0), axis=-1, keepdims=True)

    o_c3 = o_c.reshape(NSA_REP, tq, d)
    o_s3 = o_s.reshape(NSA_REP, tq, d)
    o_w3 = o_w.reshape(NSA_REP, tq, d)
    for r in range(NSA_REP):
        o_ref[0, 0, r] = gate_col(0, r) * o_c3[r] + gate_col(1, r) * o_s3[r] + gate_col(2, r) * o_w3[r]


def _prompt_bias_tables(rel_bias):
    tq = Q_BLOCK
    i = np.arange(tq)
    e = np.arange(CMP_NEAR)
    dist_c = i[:, None] + CMP_BLOCK * e[None, :] - (tq - 1)
    bkt_c = _rel_bucket_np(dist_c)
    rel = rel_bias.astype(F32)
    far = rel[REL_BUCKETS - 1]
    near = rel[bkt_c] - far
    near = near.transpose(2, 0, 1).reshape(NSA_GROUPS, NSA_REP * tq, CMP_NEAR)
    farc = jnp.broadcast_to(far.reshape(NSA_GROUPS, NSA_REP, 1, 1), (NSA_GROUPS, NSA_REP, tq, 1))
    farc = farc.reshape(NSA_GROUPS, NSA_REP * tq, 1)
    cols = list(_split3(near)) + list(_split3(farc))
    t4 = jnp.concatenate(cols, axis=-1)
    t4 = jnp.pad(t4, ((0, 0), (0, 0), (0, LANES - t4.shape[-1])))
    dl = np.arange(TOEP_TILES)
    dist_t = tq * dl[:, None, None] + i[None, :, None] - i[None, None, :]
    onehot = jax.nn.one_hot(_rel_bucket_np(dist_t).reshape(-1), REL_BUCKETS, dtype=F32)
    toep = jnp.einsum('bh,nb->hn', rel, onehot, precision=HI)
    toep = toep.reshape(NSA_GROUPS, NSA_REP, TOEP_TILES, tq, tq)
    toep = toep.transpose(0, 2, 1, 3, 4).reshape(NSA_GROUPS, TOEP_TILES, NSA_REP * tq, tq)
    return t4, toep


def nsa_prompt(p_rows, nseq, rel_bias):
    m = p_rows.shape[0]
    L = m // nseq
    tq = Q_BLOCK
    d = NSA_HEADDIM
    g = NSA_GROUPS
    prep_rows = min(PREP_ROWS, L)
    qr, kcm, vcm, ks, vs, kw, vw, kv_rows = nsa_prep(p_rows.reshape(m // prep_rows, prep_rows, IN0_PAD), nseq)
    t4, toep = _prompt_bias_tables(rel_bias)
    nq = L // tq
    ncb = L // CMP_BLOCK
    rows = NSA_REP * tq
    full = lambda n: pl.BlockSpec((1, 1, n, d), lambda s, gg, q: (s, gg, 0, 0))
    o = pl.pallas_call(
        _nsa_prompt_kernel,
        grid=(nseq, g, nq),
        in_specs=[pl.BlockSpec((1, 1, NSA_REP, tq, d), lambda s, gg, q: (s, gg, 0, q, 0)),
                  full(ncb), full(ncb), full(L), full(L), full(L), full(L),
                  pl.BlockSpec((1, tq, LANES), lambda s, gg, q: (s * nq + q, 0, COL_MISC // LANES)),
                  pl.BlockSpec((1, rows, LANES), lambda s, gg, q: (gg, 0, 0)),
                  pl.BlockSpec((1, TOEP_TILES, rows, tq), lambda s, gg, q: (gg, 0, 0, 0))],
        out_specs=pl.BlockSpec((1, 1, NSA_REP, tq, d), lambda s, gg, q: (s, gg, 0, q, 0)),
        out_shape=jax.ShapeDtypeStruct((nseq, g, NSA_REP, L, d), F32),
        scratch_shapes=[pltpu.VMEM((rows, 1), F32), pltpu.VMEM((rows, 1), F32), pltpu.VMEM((rows, d), F32)],
        compiler_params=_cparams(("parallel", "parallel", "arbitrary")),
        name="nsa_prompt",
    )(qr, kcm, vcm, ks, vs, kw, vw, p_rows.reshape(m // tq, tq, IN0_PAD), t4, toep)
    kv_rows = kv_rows.reshape(6, nseq, 1, L, g, d)
    return o.transpose(0, 3, 1, 2, 4).reshape(m, g * NSA_REP * d), [kv_rows[i] for i in range(6)]


SAMPLE_PAD = 128


def _nsa_sample_kernel(*refs, n_pages, t_new):
    pt_ref = refs[0]
    del pt_ref
    q_ref, gate_ref, kvs_ref, kvw_ref, wk_ref, wv_ref, bc_ref, bs_ref, bw_ref, ex_ref = refs[1:11]
    pages = refs[11:11 + 4 * n_pages]
    o_ref, nwk_ref, nwv_ref = refs[11 + 4 * n_pages:14 + 4 * n_pages]
    kc_scr, vc_scr, ks_scr, vs_scr, kw_scr, vw_scr = refs[14 + 4 * n_pages:]
    ck_pages = pages[0:n_pages]
    cv_pages = pages[n_pages:2 * n_pages]
    sk_pages = pages[2 * n_pages:3 * n_pages]
    sv_pages = pages[3 * n_pages:4 * n_pages]
    page = ck_pages[0].shape[2]
    past = n_pages * page
    wb = wk_ref.shape[2]
    nrow = q_ref.shape[1]
    per_page = page // CMP_BLOCK
    ncb = n_pages * per_page
    kd = NSA_GROUPS * NSA_HEADDIM

    def pad_tile(new):
        return jnp.concatenate([new, jnp.zeros((SAMPLE_PAD - t_new, kd), F32)], axis=0).astype(BF16)

    for pg in range(n_pages):
        ck = ck_pages[pg][0, 0]
        cv = cv_pages[pg][0, 0]
        kc_scr[pg * per_page:(pg + 1) * per_page, :] = jnp.mean(ck.reshape(per_page, CMP_BLOCK, kd), axis=1)
        vc_scr[pg * per_page:(pg + 1) * per_page, :] = jnp.mean(cv.reshape(per_page, CMP_BLOCK, kd), axis=1)
        ks_scr[pg * page:(pg + 1) * page, :] = sk_pages[pg][0, 0].astype(BF16)
        vs_scr[pg * page:(pg + 1) * page, :] = sv_pages[pg][0, 0].astype(BF16)
    kvs = kvs_ref[0]
    kvw = kvw_ref[0]
    ks_scr[past:past + SAMPLE_PAD, :] = pad_tile(kvs[:, :kd])
    vs_scr[past:past + SAMPLE_PAD, :] = pad_tile(kvs[:, kd:])
    wk = wk_ref[0, 0]
    wv = wv_ref[0, 0]
    kw_scr[0:wb, :] = wk.astype(BF16)
    vw_scr[0:wb, :] = wv.astype(BF16)
    kw_scr[wb:wb + SAMPLE_PAD, :] = pad_tile(kvw[:, :kd])
    vw_scr[wb:wb + SAMPLE_PAD, :] = pad_tile(kvw[:, kd:])
    nwk_ref[0, 0, 0:wb - t_new, :] = wk[t_new:, :]
    nwk_ref[0, 0, wb - t_new:wb, :] = kvw[:, :kd]
    nwv_ref[0, 0, 0:wb - t_new, :] = wv[t_new:, :]
    nwv_ref[0, 0, wb - t_new:wb, :] = kvw[:, kd:]

    q2 = q_ref[0]
    tok = _iota((nrow, 1), 0) % t_new

    p = _softmax_rows(_dot_nt(q2, kc_scr[...]) + bc_ref[...])
    o_c = _dot(p, vc_scr[...])
    ngt = NSA_GROUPS * t_new
    per_g = nrow // NSA_GROUPS
    ra = _iota((ngt, nrow), 0)
    rb = _iota((ngt, nrow), 1)
    same = (ra // t_new == rb // per_g) & (ra % t_new == rb % t_new)
    psum = _dot_hi(_ones_where(same, F32), p)
    ratio = SEL_BLOCK // CMP_BLOCK
    fold = _ones_where(_iota((ncb, LANES), 0) // ratio == _iota((ncb, LANES), 1), F32)
    imp = _dot_hi(psum, fold)
    jj = _iota((ngt, LANES), 1)
    cur = (past + _iota((ngt, LANES), 0) % t_new) // SEL_BLOCK
    forced = (jj == 0) | (jj == cur) | (jj == cur - 1)
    score = jnp.where(forced, FORCE_SCORE, jnp.where(jj <= cur, imp, NEG_INF))
    sel = _topk_mask(score, SEL_TOPK)
    ea = _iota((nrow, ngt), 0)
    eb = _iota((nrow, ngt), 1)
    spread = _ones_where((eb // t_new == ea // per_g) & (eb % t_new == ea % t_new), BF16)
    sel_rows = jnp.dot(spread, sel.astype(BF16), preferred_element_type=F32)
    chosen = jnp.dot(sel_rows.astype(BF16), ex_ref[...], preferred_element_type=F32)

    lk = past + SAMPLE_PAD
    kpos = _iota((nrow, lk), 1)
    keep = (chosen > 0.5) & (kpos <= past + tok)
    s = jnp.where(keep, _dot_nt(q2, ks_scr[...]) + bs_ref[...], NEG_INF)
    o_s = _dot(_softmax_rows(s), vs_scr[...])

    dist = wb + tok - _iota((nrow, wb + SAMPLE_PAD), 1)
    s = jnp.where((dist >= 0) & (dist < WINDOW), _dot_nt(q2, kw_scr[...]) + bw_ref[...], NEG_INF)
    o_w = _dot(_softmax_rows(s), vw_scr[...])

    gates = _sigmoid(gate_ref[0])
    o_ref[0] = gates[:, 0:1] * o_c + gates[:, 1:2] * o_s + gates[:, 2:3] * o_w


def nsa_sample(p_rows, nseq, rel_bias, caches, win_k, win_v, page_table):
    m = p_rows.shape[0]
    t = m // nseq
    d = NSA_HEADDIM
    g = NSA_GROUPS
    kd = g * d
    n_pages = page_table.shape[1]
    page = caches[0].shape[2]
    past = n_pages * page
    wb = win_k.shape[2]
    nrow = g * NSA_REP * t
    ncb = past // CMP_BLOCK
    lk = past + SAMPLE_PAD
    lw = wb + SAMPLE_PAD

    q = p_rows[:, COL_Q:COL_Q + NSA_HEADS * d].reshape(nseq, t, g, NSA_REP, d) * (d ** -0.5)
    q = q.transpose(0, 2, 3, 1, 4)
    eye_g = jnp.eye(g, dtype=F32)
    q2 = (q[:, :, :, :, None, :] * eye_g[None, :, None, None, :, None]).reshape(nseq, nrow, kd).astype(BF16)
    graw = p_rows[:, COL_MISC + SSD_HEADS:COL_MISC + SSD_HEADS + 3 * NSA_HEADS]
    graw = graw.reshape(nseq, t, 3, g, NSA_REP).transpose(0, 3, 4, 1, 2).reshape(nseq, nrow, 3)

    rel = rel_bias.astype(F32)
    tt = np.arange(t)
    head_rows = lambda tab: tab.transpose(2, 0, 1).reshape(g, NSA_REP, t, -1).reshape(nrow, -1)
    bias_c = head_rows(rel[_rel_bucket_np(past + tt[:, None] - (np.arange(ncb)[None, :] * CMP_BLOCK + CMP_BLOCK - 1))])
    bias_s = head_rows(rel[_rel_bucket_np(past + tt[:, None] - np.arange(lk)[None, :])])
    bias_w = head_rows(rel[_rel_bucket_np(wb + tt[:, None] - np.arange(lw)[None, :])])
    expand = jnp.asarray(np.arange(LANES)[:, None] == (np.arange(lk)[None, :] // SEL_BLOCK), BF16)

    p3 = p_rows.reshape(nseq, t, IN0_PAD)
    pt_flat = page_table.reshape(-1).astype(jnp.int32)
    c4 = [c.reshape(c.shape[0], c.shape[1], page, kd) for c in caches]
    wk4 = win_k.reshape(nseq, win_k.shape[1], wb, kd)
    wv4 = win_v.reshape(nseq, win_v.shape[1], wb, kd)
    const2 = lambda s, pt: (0, 0)
    in_specs = [pl.BlockSpec((1, nrow, kd), lambda s, pt: (s, 0, 0)),
                pl.BlockSpec((1, nrow, 3), lambda s, pt: (s, 0, 0)),
                pl.BlockSpec((1, t, 2 * kd), lambda s, pt: (s, 0, (COL_KV + 2 * kd) // (2 * kd))),
                pl.BlockSpec((1, t, 2 * kd), lambda s, pt: (s, 0, (COL_KV + 4 * kd) // (2 * kd))),
                pl.BlockSpec((1, 1, wb, kd), lambda s, pt: (s, 0, 0, 0)),
                pl.BlockSpec((1, 1, wb, kd), lambda s, pt: (s, 0, 0, 0)),
                pl.BlockSpec((nrow, ncb), const2), pl.BlockSpec((nrow, lk), const2),
                pl.BlockSpec((nrow, lw), const2), pl.BlockSpec((LANES, lk), const2)]
    args = [q2, graw, p3, p3, wk4, wv4, bias_c, bias_s, bias_w, expand]
    for c in c4:
        for pg in range(n_pages):
            in_specs.append(pl.BlockSpec((1, 1, page, kd),
                                         lambda s, pt, pg=pg: (pt[s * n_pages + pg], 0, 0, 0)))
            args.append(c)
    grid_spec = pltpu.PrefetchScalarGridSpec(
        num_scalar_prefetch=1,
        grid=(nseq,),
        in_specs=in_specs,
        out_specs=[pl.BlockSpec((1, nrow, kd), lambda s, pt: (s, 0, 0)),
                   pl.BlockSpec((1, 1, wb, kd), lambda s, pt: (s, 0, 0, 0)),
                   pl.BlockSpec((1, 1, wb, kd), lambda s, pt: (s, 0, 0, 0))],
        scratch_shapes=[pltpu.VMEM((ncb, kd), F32), pltpu.VMEM((ncb, kd), F32),
                        pltpu.VMEM((lk, kd), BF16), pltpu.VMEM((lk, kd), BF16),
                        pltpu.VMEM((lw, kd), BF16), pltpu.VMEM((lw, kd), BF16)])
    o, nwk, nwv = pl.pallas_call(
        functools.partial(_nsa_sample_kernel, n_pages=n_pages, t_new=t),
        grid_spec=grid_spec,
        out_shape=[jax.ShapeDtypeStruct((nseq, nrow, kd), F32),
                   jax.ShapeDtypeStruct((nseq, 1, wb, kd), F32),
                   jax.ShapeDtypeStruct((nseq, 1, wb, kd), F32)],
        compiler_params=_cparams(("arbitrary",)),
        name="nsa_sample",
    )(pt_flat, *args)
    o = o.reshape(nseq, g, NSA_REP, t, g, d)
    o = jnp.stack([o[:, gi, :, :, gi, :] for gi in range(g)], axis=1)
    o = o.transpose(0, 3, 1, 2, 4).reshape(m, g * NSA_REP * d)
    return o, nwk.reshape(nseq, 1, wb, g, d), nwv.reshape(nseq, 1, wb, g, d)


CONF_ROWS = 256


def _trunk(x, nseq, mem_k, mem_v, w, st):
    m, dmod = x.shape
    L = m // nseq
    p_rows = norm_matmul(x, w['norm_mix'][0], w['w_in0'])
    q_ssd = SSD_CHUNK if L % SSD_CHUNK == 0 else L
    p3 = p_rows.reshape(m // q_ssd, q_ssd, IN0_PAD)
    if st is None:
        y_ssd, h_pairs = ssd_mixer(p3, nseq, None, None, *w['ssd'])
        o_nsa, kv = nsa_prompt(p_rows, nseq, w['rel_bias'])
        wbuf = min(WINDOW, L)
    else:
        y_ssd, h_pairs = ssd_mixer(p3, nseq, st['ssd_conv'], _state_to_pairs(st['ssm'][:, 0]), *w['ssd'])
        o_nsa, new_wk, new_wv = nsa_sample(p_rows, nseq, w['rel_bias'], st['pages'], st['win_k'], st['win_v'],
                                           st['page_table'])
    ssm = _pairs_to_state(h_pairs)[:, None]
    pr = p_rows.reshape(nseq, L, IN0_PAD)
    dconv = SSD_HEADS * SSD_HEADDIM + 2 * SSD_GROUPS * SSD_STATE
    sconv = pr[:, L - (SSD_CONV - 1):, COL_X:COL_X + dconv][:, None]
    kd = NSA_GROUPS * NSA_HEADDIM
    if st is not None:
        kv = [pr[:, :, COL_KV + i * kd:COL_KV + (i + 1) * kd].reshape(nseq, 1, L, NSA_GROUPS, NSA_HEADDIM)
              for i in range(6)]
    if st is None:
        new_wk, new_wv = kv[4][:, :, L - wbuf:], kv[5][:, :, L - wbuf:]
    x = matmul_res([y_ssd.reshape(m, -1), o_nsa], [w['w_out0_ssd'], w['w_out0_nsa']], x)
    x = cross_attend(x, nseq, w['norm_cross'][0], w['mem_wq'][0], w['mem_wo'][0], mem_k, mem_v, 0)
    x = swiglu_ffn(x, w['norm_ffn'][0], w['ffn_wg'], w['ffn_wu'], w['ffn_wd'])
    ag = norm_matmul(x, w['norm_mix'][1], w['conf_w_in'])
    tc = CONF_ROWS if L % CONF_ROWS == 0 else L
    xo, conf_buf = conf_mixer(ag.reshape(m // tc, tc, -1), x.reshape(m // tc, tc, dmod), nseq,
                              None if st is None else st['conf_conv'], *w['conf'])
    x = xo.reshape(m, dmod)
    x = cross_attend(x, nseq, w['norm_cross'][1], w['mem_wq'][1], w['mem_wo'][1], mem_k, mem_v, 1)
    y = moe_ffn_final(x, w['norm_ffn'][1], w['moe_router'], w['moe_wg'], w['moe_wu'], w['moe_wd'], w['norm_final'])
    return (y.reshape(nseq, L, dmod), ssm, sconv, kv[0], kv[1], kv[2], kv[3], new_wk, new_wv, conf_buf[:, None])


def kernel(x_prompt, x_sample, mem_prompt, cache_mem_k, cache_mem_v, cache_nsa_cmp_k, cache_nsa_cmp_v,
           cache_nsa_sel_k, cache_nsa_sel_v, cache_nsa_win_k, cache_nsa_win_v, state_ssm, state_ssd_conv,
           state_conf_conv, page_table, norm_mix, norm_cross, norm_ffn, norm_final, w_in0, w_out0, ssd_conv_w,
           ssd_conv_b, ssd_dt_bias, ssd_a_log, ssd_d, ssd_norm, rel_bias, conf_w_in, conf_dw, conf_dw_b,
           conf_ln_g, conf_ln_b, conf_w_out, mem_wq, mem_wk, mem_wv, mem_wo, ffn_wg, ffn_wu, ffn_wd,
           moe_router, moe_wg, moe_wu, moe_wd):
    bf = lambda a: a.astype(BF16)
    d_ssd = SSD_HEADS * SSD_HEADDIM
    w = {
        'norm_mix': norm_mix, 'norm_cross': norm_cross, 'norm_ffn': norm_ffn, 'norm_final': norm_final,
        'w_in0': _reorder_in0(w_in0[0]),
        'w_out0_ssd': bf(w_out0[0, :d_ssd]), 'w_out0_nsa': bf(w_out0[0, d_ssd:]),
        'ssd': (ssd_conv_w[0], ssd_conv_b[0], ssd_dt_bias[0], ssd_a_log[0], ssd_d[0], ssd_norm[0]),
        'rel_bias': rel_bias,
        'conf_w_in': bf(conf_w_in[0]),
        'conf': (conf_dw[0], conf_dw_b[0], conf_ln_g[0], conf_ln_b[0], bf(conf_w_out[0])),
        'mem_wq': bf(mem_wq), 'mem_wo': bf(mem_wo),
        'ffn_wg': bf(ffn_wg[0]), 'ffn_wu': bf(ffn_wu[0]), 'ffn_wd': bf(ffn_wd[0]),
        'moe_router': moe_router[0], 'moe_wg': bf(moe_wg[0]), 'moe_wu': bf(moe_wu[0]), 'moe_wd': bf(moe_wd[0]),
    }
    bp, sp, dmod = x_prompt.shape
    bs, ss, _ = x_sample.shape
    mk_p, mv_p = mem_kv(mem_prompt, bf(mem_wk), bf(mem_wv))
    mem_shape = (bp, mem_wk.shape[0], mem_prompt.shape[1], MEM_HEADS, MEM_HEADDIM)
    outs_p = _trunk(x_prompt.reshape(bp * sp, dmod), bp, mk_p, mv_p, w, None)
    st = {'ssm': state_ssm, 'ssd_conv': state_ssd_conv, 'win_k': cache_nsa_win_k, 'win_v': cache_nsa_win_v,
          'conf_conv': state_conf_conv, 'page_table': page_table,
          'pages': (cache_nsa_cmp_k, cache_nsa_cmp_v, cache_nsa_sel_k, cache_nsa_sel_v)}
    outs_s = _trunk(x_sample.reshape(bs * ss, dmod), bs, cache_mem_k, cache_mem_v, w, st)
    return (outs_p[0], outs_s[0], mk_p.reshape(mem_shape), mv_p.reshape(mem_shape)) + outs_p[1:] + outs_s[1:]
```

```python
import functools
import math

import numpy as np
import jax
import jax.numpy as jnp
from jax import lax
from jax.experimental import pallas as pl
from jax.experimental.pallas import tpu as pltpu

F32 = jnp.float32
BF16 = jnp.bfloat16

SSD_HEADS = 16
SSD_HEADDIM = 64
SSD_GROUPS = 2
SSD_STATE = 128
SSD_CONV = 4
SSD_CHUNK = 128
NSA_HEADS = 16
NSA_HEADDIM = 64
NSA_GROUPS = 2
NSA_REP = 8
CMP_BLOCK = 32
SEL_BLOCK = 64
SEL_TOPK = 16
WINDOW = 512
Q_BLOCK = 128
REL_BUCKETS = 32
REL_MAX_DIST = 1024
CONF_KERNEL = 31
MEM_HEADS = 4
MEM_HEADDIM = 128
N_EXPERTS = 8
RMS_EPS = 1e-6
LN_EPS = 1e-5
NEG_INF = -1e30
FORCE_SCORE = 1e30

LANES = 128
SUBLANES = 8
VMEM_LIMIT = 56 * 1024 * 1024

COL_Z = 0
COL_X = 1024
COL_BC = 2048
COL_Q = 2560
COL_KV = 3584
COL_MISC = 4352
IN0_PAD = 4608

HI = lax.Precision.HIGHEST


def _cparams(sem):
    return pltpu.CompilerParams(dimension_semantics=sem, vmem_limit_bytes=VMEM_LIMIT)


def _pick(n, pref):
    for t in pref:
        if n % t == 0:
            return t
    return n


def _silu(x):
    return x * (1.0 / (1.0 + jnp.exp(-x)))


def _sigmoid(x):
    return 1.0 / (1.0 + jnp.exp(-x))


def _dot(a, b):
    return jnp.dot(a.astype(BF16), b.astype(BF16), preferred_element_type=F32)


def _dot_nt(a, b):
    return lax.dot_general(a.astype(BF16), b.astype(BF16), (((1,), (1,)), ((), ())),
                           preferred_element_type=F32)


def _dot_tn(a, b):
    return lax.dot_general(a.astype(BF16), b.astype(BF16), (((0,), (0,)), ((), ())),
                           preferred_element_type=F32)


def _dot_hi(a, b):
    return jnp.dot(a, b, precision=HI, preferred_element_type=F32)


def _dot_nt_hi(a, b):
    return lax.dot_general(a, b, (((1,), (1,)), ((), ())), precision=HI, preferred_element_type=F32)


def _iota(shape, dim):
    return lax.broadcasted_iota(jnp.int32, shape, dim)


def _ones_where(cond, dtype):
    return jnp.where(cond, 1.0, 0.0).astype(dtype)


def _rms(x, g):
    return x * lax.rsqrt(jnp.mean(x * x, axis=-1, keepdims=True) + RMS_EPS) * g


def _norm_matmul_kernel(x_ref, g_ref, w_ref, o_ref, xn_ref):
    @pl.when(pl.program_id(1) == 0)
    def _():
        xn_ref[...] = _rms(x_ref[...], g_ref[...]).astype(BF16)

    o_ref[...] = jnp.dot(xn_ref[...], w_ref[...], preferred_element_type=F32)


def norm_matmul(x, g, w):
    m, k = x.shape
    n = w.shape[1]
    tm = _pick(m, (512, 256, 128, 64, 32, 16, 8))
    tn = _pick(n, (512, 256, 128))
    return pl.pallas_call(
        _norm_matmul_kernel,
        grid=(m // tm, n // tn),
        in_specs=[pl.BlockSpec((tm, k), lambda i, j: (i, 0)),
                  pl.BlockSpec((1, k), lambda i, j: (0, 0)),
                  pl.BlockSpec((k, tn), lambda i, j: (0, j))],
        out_specs=pl.BlockSpec((tm, tn), lambda i, j: (i, j)),
        out_shape=jax.ShapeDtypeStruct((m, n), F32),
        scratch_shapes=[pltpu.VMEM((tm, k), BF16)],
        compiler_params=_cparams(("parallel", "arbitrary")),
        name="norm_matmul",
    )(x, g.reshape(1, k), w)


def _matmul_res_kernel(*refs, n_in):
    a_refs = refs[:n_in]
    w_refs = refs[n_in:2 * n_in]
    r_ref, o_ref = refs[2 * n_in], refs[2 * n_in + 1]
    acc = r_ref[...]
    for a_ref, w_ref in zip(a_refs, w_refs):
        acc = acc + jnp.dot(a_ref[...].astype(BF16), w_ref[...], preferred_element_type=F32)
    o_ref[...] = acc


def matmul_res(a_list, w_list, r):
    m, n = r.shape
    tm = _pick(m, (512, 256, 128, 64, 32, 16, 8))
    tn = _pick(n, (512, 256, 128))
    n_in = len(a_list)
    in_specs = ([pl.BlockSpec((tm, a.shape[1]), lambda i, j: (i, 0)) for a in a_list]
                + [pl.BlockSpec((w.shape[0], tn), lambda i, j: (0, j)) for w in w_list]
                + [pl.BlockSpec((tm, tn), lambda i, j: (i, j))])
    return pl.pallas_call(
        functools.partial(_matmul_res_kernel, n_in=n_in),
        grid=(m // tm, n // tn),
        in_specs=in_specs,
        out_specs=pl.BlockSpec((tm, tn), lambda i, j: (i, j)),
        out_shape=jax.ShapeDtypeStruct((m, n), F32),
        compiler_params=_cparams(("parallel", "arbitrary")),
        name="matmul_res",
    )(*a_list, *w_list, r)


def _swiglu_kernel(x_ref, g_ref, wg_ref, wu_ref, wd_ref, o_ref, xn_ref, acc_ref):
    f = pl.program_id(1)

    @pl.when(f == 0)
    def _():
        xn_ref[...] = _rms(x_ref[...], g_ref[...]).astype(BF16)
        acc_ref[...] = jnp.zeros_like(acc_ref)

    xn = xn_ref[...]
    hg = jnp.dot(xn, wg_ref[...], preferred_element_type=F32)
    hu = jnp.dot(xn, wu_ref[...], preferred_element_type=F32)
    acc_ref[...] += jnp.dot((_silu(hg) * hu).astype(BF16), wd_ref[...], preferred_element_type=F32)

    @pl.when(f == pl.num_programs(1) - 1)
    def _():
        o_ref[...] = x_ref[...] + acc_ref[...]


def swiglu_ffn(x, g, wg, wu, wd):
    m, d = x.shape
    ff = wg.shape[1]
    tm = _pick(m, (512, 256, 128, 64, 32, 16, 8))
    tf = _pick(ff, (512, 256, 128))
    return pl.pallas_call(
        _swiglu_kernel,
        grid=(m // tm, ff // tf),
        in_specs=[pl.BlockSpec((tm, d), lambda i, f: (i, 0)),
                  pl.BlockSpec((1, d), lambda i, f: (0, 0)),
                  pl.BlockSpec((d, tf), lambda i, f: (0, f)),
                  pl.BlockSpec((d, tf), lambda i, f: (0, f)),
                  pl.BlockSpec((tf, d), lambda i, f: (f, 0))],
        out_specs=pl.BlockSpec((tm, d), lambda i, f: (i, 0)),
        out_shape=jax.ShapeDtypeStruct((m, d), F32),
        scratch_shapes=[pltpu.VMEM((tm, d), BF16), pltpu.VMEM((tm, d), F32)],
        compiler_params=_cparams(("parallel", "arbitrary")),
        name="swiglu_ffn",
    )(x, g.reshape(1, d), wg, wu, wd)


def _top2_gate(logits):
    lane = _iota(logits.shape, 1)
    m1 = jnp.max(logits, axis=-1, keepdims=True)
    i1 = jnp.min(jnp.where(logits == m1, lane, LANES), axis=-1, keepdims=True)
    rest = jnp.where(lane == i1, -jnp.inf, logits)
    m2 = jnp.max(rest, axis=-1, keepdims=True)
    i2 = jnp.min(jnp.where(rest == m2, lane, LANES), axis=-1, keepdims=True)
    e2 = jnp.exp(m2 - m1)
    den = 1.0 + e2
    return jnp.where(lane == i1, 1.0 / den, 0.0) + jnp.where(lane == i2, e2 / den, 0.0)


def _moe_kernel(x_ref, g_ref, wr_ref, wg_ref, wu_ref, wd_ref, gf_ref, o_ref, xn_ref, gate_ref, acc_ref):
    e = pl.program_id(1)
    f = pl.program_id(2)

    @pl.when((e == 0) & (f == 0))
    def _():
        xn = _rms(x_ref[...], g_ref[...]).astype(BF16)
        xn_ref[...] = xn
        logits = jnp.dot(xn, wr_ref[...], preferred_element_type=F32)
        lane = _iota(logits.shape, 1)
        gate_ref[...] = _top2_gate(jnp.where(lane < N_EXPERTS, logits, -jnp.inf))
        acc_ref[...] = jnp.zeros_like(acc_ref)

    xn = xn_ref[...]
    gate = gate_ref[...]
    lane = _iota(gate.shape, 1)
    ge = jnp.sum(jnp.where(lane == e, gate, 0.0), axis=-1, keepdims=True)
    hg = jnp.dot(xn, wg_ref[0], preferred_element_type=F32)
    hu = jnp.dot(xn, wu_ref[0], preferred_element_type=F32)
    y = jnp.dot((_silu(hg) * hu).astype(BF16), wd_ref[0], preferred_element_type=F32)
    acc_ref[...] += ge * y

    @pl.when((e == pl.num_programs(1) - 1) & (f == pl.num_programs(2) - 1))
    def _():
        o_ref[...] = _rms(x_ref[...] + acc_ref[...], gf_ref[...])


def moe_ffn_final(x, g, w_router, wg, wu, wd, g_final):
    m, d = x.shape
    ne, _, ff = wg.shape
    tm = _pick(m, (512, 256, 128, 64, 32, 16, 8))
    tf = _pick(ff, (512, 256, 128))
    wr = jnp.zeros((d, LANES), BF16).at[:, :ne].set(w_router.astype(BF16))
    return pl.pallas_call(
        _moe_kernel,
        grid=(m // tm, ne, ff // tf),
        in_specs=[pl.BlockSpec((tm, d), lambda i, e, f: (i, 0)),
                  pl.BlockSpec((1, d), lambda i, e, f: (0, 0)),
                  pl.BlockSpec((d, LANES), lambda i, e, f: (0, 0)),
                  pl.BlockSpec((1, d, tf), lambda i, e, f: (e, 0, f)),
                  pl.BlockSpec((1, d, tf), lambda i, e, f: (e, 0, f)),
                  pl.BlockSpec((1, tf, d), lambda i, e, f: (e, f, 0)),
                  pl.BlockSpec((1, d), lambda i, e, f: (0, 0))],
        out_specs=pl.BlockSpec((tm, d), lambda i, e, f: (i, 0)),
        out_shape=jax.ShapeDtypeStruct((m, d), F32),
        scratch_shapes=[pltpu.VMEM((tm, d), BF16), pltpu.VMEM((tm, LANES), F32), pltpu.VMEM((tm, d), F32)],
        compiler_params=_cparams(("parallel", "arbitrary", "arbitrary")),
        name="moe_ffn",
    )(x, g.reshape(1, d), wr, wg, wu, wd, g_final.reshape(1, d))


def _mem_kv_kernel(x_ref, wk_ref, wv_ref, k_ref, v_ref):
    x = x_ref[0].astype(BF16)
    k_ref[0, 0] = jnp.dot(x, wk_ref[0], preferred_element_type=F32)
    v_ref[0, 0] = jnp.dot(x, wv_ref[0], preferred_element_type=F32)


def mem_kv(mem, wk, wv):
    b, t, d = mem.shape
    nl, _, e = wk.shape
    out = jax.ShapeDtypeStruct((b, nl, t, e), F32)
    return pl.pallas_call(
        _mem_kv_kernel,
        grid=(b, nl),
        in_specs=[pl.BlockSpec((1, t, d), lambda i, l: (i, 0, 0)),
                  pl.BlockSpec((1, d, e), lambda i, l: (l, 0, 0)),
                  pl.BlockSpec((1, d, e), lambda i, l: (l, 0, 0))],
        out_specs=[pl.BlockSpec((1, 1, t, e), lambda i, l: (i, l, 0, 0)),
                   pl.BlockSpec((1, 1, t, e), lambda i, l: (i, l, 0, 0))],
        out_shape=[out, out],
        compiler_params=_cparams(("parallel", "arbitrary")),
        name="mem_kv",
    )(mem, wk, wv)


def _cross_attn_kernel(q_ref, k_ref, v_ref, o_ref):
    q = q_ref[0] * (MEM_HEADDIM ** -0.5)
    k = k_ref[0, 0]
    v = v_ref[0, 0]
    outs = []
    for h in range(MEM_HEADS):
        sl = slice(h * MEM_HEADDIM, (h + 1) * MEM_HEADDIM)
        s = _dot_nt(q[:, sl], k[:, sl])
        p = jnp.exp(s - jnp.max(s, axis=-1, keepdims=True))
        p = p / jnp.sum(p, axis=-1, keepdims=True)
        outs.append(_dot(p, v[:, sl]))
    o_ref[0] = jnp.concatenate(outs, axis=-1)


def cross_attn_core(q, mk, mv, layer):
    s, t, e = q.shape
    mt = mk.shape[2]
    tq = _pick(t, (512, 256, 128)) if t >= 128 else t
    return pl.pallas_call(
        _cross_attn_kernel,
        grid=(s, t // tq),
        in_specs=[pl.BlockSpec((1, tq, e), lambda i, j: (i, j, 0)),
                  pl.BlockSpec((1, 1, mt, e), lambda i, j: (i, layer, 0, 0)),
                  pl.BlockSpec((1, 1, mt, e), lambda i, j: (i, layer, 0, 0))],
        out_specs=pl.BlockSpec((1, tq, e), lambda i, j: (i, j, 0)),
        out_shape=jax.ShapeDtypeStruct((s, t, e), F32),
        compiler_params=_cparams(("parallel", "arbitrary")),
        name="cross_attn",
    )(q, mk, mv)


def cross_attend(x, nseq, g, wq, wo, mk, mv, layer):
    m, d = x.shape
    e = wq.shape[1]
    q = norm_matmul(x, g, wq)
    mk = mk.reshape(mk.shape[0], mk.shape[1], mk.shape[2], e)
    mv = mv.reshape(mk.shape)
    o = cross_attn_core(q.reshape(nseq, m // nseq, e), mk, mv, layer)
    return matmul_res([o.reshape(m, e)], [wo], x)


def _ssd_kernel(*refs, lreal, has_state):
    if has_state:
        (z_ref, x_ref, bc_ref, dt_ref, hist_ref, h0_ref, cw_ref, cb_ref, dtb_ref, alog_ref, d_ref, nw_ref,
         y_ref, hout_ref, xp_scr, st_scr, y_scr) = refs
    else:
        (z_ref, x_ref, bc_ref, dt_ref, cw_ref, cb_ref, dtb_ref, alog_ref, d_ref, nw_ref,
         y_ref, hout_ref, xp_scr, st_scr, y_scr) = refs
    c = pl.program_id(1)
    q8 = xp_scr.shape[0] - SUBLANES
    dx = x_ref.shape[-1]
    n = SSD_STATE
    hist = SSD_CONV - 1

    @pl.when(c == 0)
    def _():
        xp_scr[...] = jnp.zeros_like(xp_scr)
        if has_state:
            xp_scr[SUBLANES - hist:SUBLANES, :] = hist_ref[0, 0]
            st_scr[...] = h0_ref[0]
        else:
            st_scr[...] = jnp.zeros_like(st_scr)

    xp_scr[SUBLANES:SUBLANES + lreal, :dx] = x_ref[0]
    xp_scr[SUBLANES:SUBLANES + lreal, dx:] = bc_ref[0]
    conv = cb_ref[...]
    for k in range(SSD_CONV):
        lo = SUBLANES - hist + k
        conv = conv + cw_ref[k:k + 1, :] * xp_scr[lo:lo + q8, :]
    xc = _silu(conv)
    if q8 == lreal:
        xp_scr[0:SUBLANES, :] = xp_scr[q8:q8 + SUBLANES, :]

    lane = _iota((q8, LANES), 1)
    row = _iota((q8, LANES), 0)
    dt = jax.nn.softplus(dt_ref[0] + dtb_ref[...]) if q8 == lreal else None
    if dt is None:
        dtr = jnp.concatenate([dt_ref[0], jnp.zeros((q8 - lreal, LANES), F32)], axis=0)
        dt = jax.nn.softplus(dtr + dtb_ref[...])
    dt = jnp.where((lane < SSD_HEADS) & (row < lreal), dt, 0.0)
    ad = dt * (-jnp.exp(alog_ref[...]))
    tri = (_iota((q8, q8), 0) >= _iota((q8, q8), 1))
    acum = _dot_hi(tri.astype(F32), ad)
    eye = (_iota((LANES, LANES), 0) == _iota((LANES, LANES), 1)).astype(F32)
    acum_t = _dot_nt_hi(eye, acum)
    alast = acum[q8 - 1:q8, :]
    lo_half = lane < SSD_HEADDIM

    for g in range(SSD_GROUPS):
        bg = xc[:, dx + g * n:dx + (g + 1) * n]
        cg = xc[:, dx + (SSD_GROUPS + g) * n:dx + (SSD_GROUPS + g + 1) * n]
        cb = _dot_nt(cg, bg)
        hpg = SSD_HEADS // SSD_GROUPS
        for jp in range(hpg // 2):
            pair = g * (hpg // 2) + jp
            ha, hb = 2 * pair, 2 * pair + 1

            def sel(col_a, col_b):
                return jnp.where(lo_half, col_a, col_b)

            xs = xc[:, pair * LANES:(pair + 1) * LANES]
            xd = xs * sel(dt[:, ha:ha + 1], dt[:, hb:hb + 1])
            ys = []
            for h in (ha, hb):
                diff = acum[:, h:h + 1] - acum_t[h:h + 1, :]
                lmat = jnp.exp(jnp.where(tri, diff, NEG_INF))
                ys.append(_dot(cb * lmat, xd))
            y_diag = jnp.where(lo_half, ys[0], ys[1])
            st = st_scr[pair]
            y_off = _dot(cg, st) * sel(jnp.exp(acum[:, ha:ha + 1]), jnp.exp(acum[:, hb:hb + 1]))
            ds = sel(jnp.exp(alast[:, ha:ha + 1] - acum[:, ha:ha + 1]),
                     jnp.exp(alast[:, hb:hb + 1] - acum[:, hb:hb + 1]))
            dec = jnp.where(lo_half[0:1], jnp.exp(alast[:, ha:ha + 1]), jnp.exp(alast[:, hb:hb + 1]))
            st_scr[pair] = dec * st + _dot_tn(bg, xd * ds)
            dsk = jnp.where(lo_half[0:1], d_ref[:, ha:ha + 1], d_ref[:, hb:hb + 1])
            y_scr[:, pair * LANES:(pair + 1) * LANES] = y_diag + y_off + xs * dsk

    y = y_scr[...]
    if q8 != lreal:
        z = jnp.concatenate([z_ref[0], jnp.zeros((q8 - lreal, dx), F32)], axis=0)
    else:
        z = z_ref[0]
    y = y * _silu(z)
    half = dx // SSD_GROUPS
    lane_x = _iota((q8, dx), 1)
    ms0 = jnp.mean(jnp.square(y[:, :half]), axis=-1, keepdims=True)
    ms1 = jnp.mean(jnp.square(y[:, half:]), axis=-1, keepdims=True)
    scale = jnp.where(lane_x < half, lax.rsqrt(ms0 + RMS_EPS), lax.rsqrt(ms1 + RMS_EPS))
    y_ref[0] = (y * scale * nw_ref[...])[:lreal]

    @pl.when(c == pl.num_programs(1) - 1)
    def _():
        hout_ref[0] = st_scr[...]


def _pad_lanes(v):
    return jnp.zeros((1, LANES), F32).at[0, :v.shape[0]].set(v.astype(F32))


def ssd_mixer(p3, nseq, conv_hist, h0, cw, cb, dt_bias, a_log, d_skip, norm_w):
    nblk, q, _ = p3.shape
    nc = nblk // nseq
    dx = SSD_HEADS * SSD_HEADDIM
    dbc = 2 * SSD_GROUPS * SSD_STATE
    q8 = -(-q // SUBLANES) * SUBLANES
    npair = SSD_HEADS // 2
    has_state = h0 is not None
    row = lambda s, c: (s * nc + c, 0, 0)
    in_specs = [pl.BlockSpec((1, q, dx), lambda s, c: (s * nc + c, 0, COL_Z // dx)),
                pl.BlockSpec((1, q, dx), lambda s, c: (s * nc + c, 0, COL_X // dx)),
                pl.BlockSpec((1, q, dbc), lambda s, c: (s * nc + c, 0, COL_BC // dbc)),
                pl.BlockSpec((1, q, LANES), lambda s, c: (s * nc + c, 0, COL_MISC // LANES))]
    args = [p3, p3, p3, p3]
    if has_state:
        in_specs += [pl.BlockSpec((1, 1, SSD_CONV - 1, dx + dbc), lambda s, c: (s, 0, 0, 0)),
                     pl.BlockSpec((1, npair, SSD_STATE, LANES), lambda s, c: (s, 0, 0, 0))]
        args += [conv_hist, h0]
    const = lambda s, c: (0, 0)
    in_specs += [pl.BlockSpec((SSD_CONV, dx + dbc), const), pl.BlockSpec((1, dx + dbc), const),
                 pl.BlockSpec((1, LANES), const), pl.BlockSpec((1, LANES), const),
                 pl.BlockSpec((1, LANES), const), pl.BlockSpec((1, dx), const)]
    args += [cw, cb.reshape(1, -1), _pad_lanes(dt_bias), _pad_lanes(a_log), _pad_lanes(d_skip),
             norm_w.reshape(1, dx)]
    del row
    return pl.pallas_call(
        functools.partial(_ssd_kernel, lreal=q, has_state=has_state),
        grid=(nseq, nc),
        in_specs=in_specs,
        out_specs=[pl.BlockSpec((1, q, dx), lambda s, c: (s * nc + c, 0, 0)),
                   pl.BlockSpec((1, npair, SSD_STATE, LANES), lambda s, c: (s, 0, 0, 0))],
        out_shape=[jax.ShapeDtypeStruct((nblk, q, dx), F32),
                   jax.ShapeDtypeStruct((nseq, npair, SSD_STATE, LANES), F32)],
        scratch_shapes=[pltpu.VMEM((SUBLANES + q8, dx + dbc), F32),
                        pltpu.VMEM((npair, SSD_STATE, LANES), F32),
                        pltpu.VMEM((q8, dx), F32)],
        compiler_params=_cparams(("parallel", "arbitrary")),
        name="ssd_mixer",
    )(*args)


def _state_to_pairs(h):
    s, nh, p, n = h.shape
    return h.reshape(s, nh // 2, 2, p, n).transpose(0, 1, 4, 2, 3).reshape(s, nh // 2, n, 2 * p)


def _pairs_to_state(hp):
    s, npair, n, pp = hp.shape
    return hp.reshape(s, npair, n, 2, pp // 2).transpose(0, 1, 3, 4, 2).reshape(s, 2 * npair, pp // 2, n)


def _reorder_in0(w):
    d = w.shape[0]
    dz = SSD_HEADS * SSD_HEADDIM
    dxbc = dz + 2 * SSD_GROUPS * SSD_STATE
    o_dt = dz + dxbc
    o_q = o_dt + SSD_HEADS
    o_kv = o_q + NSA_HEADS * NSA_HEADDIM
    o_g = o_kv + 6 * NSA_GROUPS * NSA_HEADDIM
    parts = [w[:, :o_dt], w[:, o_q:o_g], w[:, o_dt:o_q], w[:, o_g:]]
    wn = jnp.concatenate(parts, axis=1)
    return jnp.pad(wn, ((0, 0), (0, IN0_PAD - wn.shape[1]))).astype(BF16)


CONF_HALO = 32


def _conf_kernel(*refs, lreal, has_state):
    if has_state:
        ag_ref, x_ref, hist_ref, dw_ref, dwb_ref, lg_ref, lb_ref, wo_ref, o_ref, buf_ref, up_scr = refs
    else:
        ag_ref, x_ref, dw_ref, dwb_ref, lg_ref, lb_ref, wo_ref, o_ref, buf_ref, up_scr = refs
    c = pl.program_id(1)
    ch = x_ref.shape[-1]
    t8 = up_scr.shape[0] - CONF_HALO
    nh = CONF_KERNEL - 1

    @pl.when(c == 0)
    def _():
        up_scr[...] = jnp.zeros_like(up_scr)
        if has_state:
            up_scr[CONF_HALO - nh:CONF_HALO, :] = hist_ref[0, 0]

    ag = ag_ref[0]
    up_scr[CONF_HALO:CONF_HALO + lreal, :] = ag[:, :ch] * _sigmoid(ag[:, ch:])
    acc = dwb_ref[...] + dw_ref[0:1, :] * up_scr[CONF_HALO - nh:CONF_HALO - nh + t8, :]
    for k in range(1, CONF_KERNEL):
        lo = CONF_HALO - nh + k
        acc = acc + dw_ref[k:k + 1, :] * up_scr[lo:lo + t8, :]
    mu = jnp.mean(acc, axis=-1, keepdims=True)
    cen = acc - mu
    var = jnp.mean(cen * cen, axis=-1, keepdims=True)
    y = _silu(cen * lax.rsqrt(var + LN_EPS) * lg_ref[...] + lb_ref[...])
    o_ref[0] = x_ref[0] + jnp.dot(y.astype(BF16), wo_ref[...], preferred_element_type=F32)[:lreal]

    @pl.when(c == pl.num_programs(1) - 1)
    def _():
        buf_ref[0] = up_scr[CONF_HALO + lreal - nh:CONF_HALO + lreal, :]

    if t8 == lreal:
        up_scr[0:CONF_HALO, :] = up_scr[t8:t8 + CONF_HALO, :]


def conf_mixer(ag3, x3, nseq, hist, dw, dwb, ln_g, ln_b, w_out):
    nblk, t, ch = x3.shape
    nc = nblk // nseq
    t8 = -(-t // SUBLANES) * SUBLANES
    nh = CONF_KERNEL - 1
    has_state = hist is not None
    const = lambda s, c: (0, 0)
    in_specs = [pl.BlockSpec((1, t, 2 * ch), lambda s, c: (s * nc + c, 0, 0)),
                pl.BlockSpec((1, t, ch), lambda s, c: (s * nc + c, 0, 0))]
    args = [ag3, x3]
    if has_state:
        in_specs.append(pl.BlockSpec((1, 1, nh, ch), lambda s, c: (s, 0, 0, 0)))
        args.append(hist)
    in_specs += [pl.BlockSpec((CONF_KERNEL, ch), const), pl.BlockSpec((1, ch), const),
                 pl.BlockSpec((1, ch), const), pl.BlockSpec((1, ch), const), pl.BlockSpec((ch, ch), const)]
    args += [dw, dwb.reshape(1, ch), ln_g.reshape(1, ch), ln_b.reshape(1, ch), w_out]
    return pl.pallas_call(
        functools.partial(_conf_kernel, lreal=t, has_state=has_state),
        grid=(nseq, nc),
        in_specs=in_specs,
        out_specs=[pl.BlockSpec((1, t, ch), lambda s, c: (s * nc + c, 0, 0)),
                   pl.BlockSpec((1, nh, ch), lambda s, c: (s, 0, 0))],
        out_shape=[jax.ShapeDtypeStruct((nblk, t, ch), F32), jax.ShapeDtypeStruct((nseq, nh, ch), F32)],
        scratch_shapes=[pltpu.VMEM((CONF_HALO + t8, ch), F32)],
        compiler_params=_cparams(("parallel", "arbitrary")),
        name="conf_mixer",
    )(*args)


def _rel_bucket_np(dist):
    n = np.maximum(dist, 0)
    max_exact = REL_BUCKETS // 2
    nf = np.maximum(n, max_exact).astype(np.float32)
    ratio = np.log(nf / np.float32(max_exact)) / np.float32(math.log(REL_MAX_DIST / max_exact))
    large = max_exact + (ratio * np.float32(REL_BUCKETS - max_exact)).astype(np.int32)
    return np.where(n < max_exact, n, np.minimum(large, REL_BUCKETS - 1)).astype(np.int32)


_FAR_DIST = int(np.argmax(_rel_bucket_np(np.arange(4 * REL_MAX_DIST)) == REL_BUCKETS - 1))
TOEP_TILES = -(-(_FAR_DIST + Q_BLOCK - 1) // Q_BLOCK) + 1
CMP_NEAR = 32
assert CMP_BLOCK * (CMP_NEAR - 3) - (Q_BLOCK - 1) >= _FAR_DIST


def _split3(x):
    hi = x.astype(BF16)
    r1 = x - hi.astype(F32)
    mid = r1.astype(BF16)
    lo = (r1 - mid.astype(F32)).astype(BF16)
    return hi, mid, lo


def _topk_mask(score, k, axis=1):
    nl = score.shape[axis]
    lane = _iota(score.shape, axis)
    sel = jnp.zeros(score.shape, F32)
    for _ in range(k):
        m = jnp.max(score, axis=axis, keepdims=True)
        idx = jnp.min(jnp.where(score == m, lane, nl), axis=axis, keepdims=True)
        pick = lane == idx
        sel = jnp.where(pick, 1.0, sel)
        score = jnp.where(pick, -jnp.inf, score)
    return sel


def _softmax_rows(s):
    p = jnp.exp(s - jnp.max(s, axis=-1, keepdims=True))
    return p * (1.0 / jnp.sum(p, axis=-1, keepdims=True))


PREP_ROWS = 512


def _nsa_prep_kernel(q0_ref, q1_ref, kvc_ref, kvs_ref, kvw_ref, qr_ref, kcm_ref, vcm_ref, ks_ref, vs_ref,
                     kw_ref, vw_ref, rows_ref):
    d = NSA_HEADDIM
    for i, src in enumerate((kvc_ref, kvs_ref, kvw_ref)):
        rows_ref[2 * i, 0] = src[0][:, :NSA_GROUPS * d]
        rows_ref[2 * i + 1, 0] = src[0][:, NSA_GROUPS * d:]
    scale = d ** -0.5
    for g, q_ref in enumerate((q0_ref, q1_ref)):
        q = q_ref[0]
        for r in range(NSA_REP):
            qr_ref[0, g, r] = (q[:, r * d:(r + 1) * d] * scale).astype(BF16)
    kvc = kvc_ref[0]
    t = kvc.shape[0]
    means = jnp.mean(kvc.reshape(t // CMP_BLOCK, CMP_BLOCK, kvc.shape[1]), axis=1)
    kvs = kvs_ref[0]
    kvw = kvw_ref[0]
    for g in range(NSA_GROUPS):
        kcm_ref[0, g] = means[:, g * d:(g + 1) * d].astype(BF16)
        vcm_ref[0, g] = means[:, (NSA_GROUPS + g) * d:(NSA_GROUPS + g + 1) * d].astype(BF16)
        ks_ref[0, g] = kvs[:, g * d:(g + 1) * d].astype(BF16)
        vs_ref[0, g] = kvs[:, (NSA_GROUPS + g) * d:(NSA_GROUPS + g + 1) * d].astype(BF16)
        kw_ref[0, g] = kvw[:, g * d:(g + 1) * d].astype(BF16)
        vw_ref[0, g] = kvw[:, (NSA_GROUPS + g) * d:(NSA_GROUPS + g + 1) * d].astype(BF16)


def nsa_prep(p3, nseq):
    nblk, t, _ = p3.shape
    nt = nblk // nseq
    L = nt * t
    d = NSA_HEADDIM
    g = NSA_GROUPS
    qw = NSA_REP * d
    kvw = 2 * g * d
    blk = lambda width, col: pl.BlockSpec((1, t, width), lambda s, c: (s * nt + c, 0, col // width))
    seq4 = lambda rows: pl.BlockSpec((1, g, rows, d), lambda s, c: (s, 0, c, 0))
    return pl.pallas_call(
        _nsa_prep_kernel,
        grid=(nseq, nt),
        in_specs=[blk(qw, COL_Q), blk(qw, COL_Q + qw), blk(kvw, COL_KV), blk(kvw, COL_KV + kvw),
                  blk(kvw, COL_KV + 2 * kvw)],
        out_specs=[pl.BlockSpec((1, g, NSA_REP, t, d), lambda s, c: (s, 0, 0, c, 0)),
                   seq4(t // CMP_BLOCK), seq4(t // CMP_BLOCK), seq4(t), seq4(t), seq4(t), seq4(t),
                   pl.BlockSpec((6, 1, t, g * d), lambda s, c: (0, s * nt + c, 0, 0))],
        out_shape=[jax.ShapeDtypeStruct((nseq, g, NSA_REP, L, d), BF16),
                   jax.ShapeDtypeStruct((nseq, g, L // CMP_BLOCK, d), BF16),
                   jax.ShapeDtypeStruct((nseq, g, L // CMP_BLOCK, d), BF16)]
                  + [jax.ShapeDtypeStruct((nseq, g, L, d), BF16)] * 4
                  + [jax.ShapeDtypeStruct((6, nblk, t, g * d), F32)],
        compiler_params=_cparams(("parallel", "arbitrary")),
        name="nsa_prep",
    )(p3, p3, p3, p3, p3)


SEL_KEYS = 512
WIN_KEYS = WINDOW + Q_BLOCK


def _nsa_prompt_kernel(qr_ref, kcm_ref, vcm_ref, ks_ref, vs_ref, kw_ref, vw_ref, misc_ref, t4_ref, toep_ref,
                       o_ref, m_scr, l_scr, acc_scr):
    g = pl.program_id(1)
    qi = pl.program_id(2)
    tq = Q_BLOCK
    rows = NSA_REP * tq
    d = NSA_HEADDIM
    ncb = kcm_ref.shape[2]
    nsb = ncb * CMP_BLOCK // SEL_BLOCK
    L = ks_ref.shape[2]
    q2 = qr_ref[0, 0].reshape(rows, d)
    tok = _iota((rows, 1), 0) % tq
    qpos = qi * tq + tok

    s = _dot_nt(q2, kcm_ref[0, 0])
    kk = t4_ref.shape[2]
    e_row = _iota((kk, ncb), 0)
    c_col = _iota((kk, ncb), 1)
    near = 3 * CMP_NEAR
    shift = (e_row % CMP_NEAR) + c_col == qi * (tq // CMP_BLOCK) + 3
    p4 = _ones_where(((e_row < near) & shift) | ((e_row >= near) & (e_row < near + 3)), BF16)
    s = s + jnp.dot(t4_ref[0], p4, preferred_element_type=F32)
    cmp_end = _iota((rows, ncb), 1) * CMP_BLOCK + (CMP_BLOCK - 1)
    valid_c = qpos >= cmp_end
    s = jnp.where(valid_c, s, NEG_INF)
    p = _softmax_rows(s)
    p = jnp.where(valid_c, p, 0.0)
    o_c = _dot(p, vcm_ref[0, 0])
    psum = jnp.sum(p.reshape(NSA_REP, tq, ncb), axis=0)
    ratio = SEL_BLOCK // CMP_BLOCK
    fold_t = (_iota((nsb, ncb), 1) // ratio == _iota((nsb, ncb), 0)).astype(F32)
    imp_t = _dot_nt_hi(fold_t, psum)

    jj = _iota((nsb, tq), 0)
    cur = (qi * tq + _iota((nsb, tq), 1)) // SEL_BLOCK
    forced = (jj == 0) | (jj == cur) | (jj == cur - 1)
    score = jnp.where(forced, FORCE_SCORE, jnp.where(jj <= cur, imp_t, NEG_INF))
    sel = _topk_mask(score, min(SEL_TOPK, nsb), axis=0).astype(BF16)

    m_scr[...] = jnp.full(m_scr.shape, NEG_INF, F32)
    l_scr[...] = jnp.zeros(l_scr.shape, F32)
    acc_scr[...] = jnp.zeros(acc_scr.shape, F32)
    tk = min(SEL_KEYS, L)
    sub = tk // tq
    qpos_t = qi * tq + _iota((tq, tk), 0)

    def sel_step(kt, carry):
        k0 = pl.multiple_of(kt * tk, tk)
        sc = _dot_nt(q2, ks_ref[0, 0, pl.ds(k0, tk), :])
        parts = []
        for u in range(sub):
            delta = jnp.clip(qi - (kt * sub + u), 0, TOEP_TILES - 1)
            parts.append(sc[:, u * tq:(u + 1) * tq] + toep_ref[0, delta])
        sc = jnp.concatenate(parts, axis=1) if sub > 1 else parts[0]
        kpos = k0 + _iota((tq, tk), 1)
        expand = _ones_where(_iota((nsb, tk), 0) == (k0 + _iota((nsb, tk), 1)) // SEL_BLOCK, BF16)
        chosen = lax.dot_general(sel, expand, (((0,), (0,)), ((), ())),
                                 preferred_element_type=F32)
        keep = (chosen > 0.5) & (kpos <= qpos_t)
        sc = jnp.where(keep[None], sc.reshape(NSA_REP, tq, tk), NEG_INF).reshape(rows, tk)
        m_old = m_scr[...]
        m_new = jnp.maximum(m_old, jnp.max(sc, axis=-1, keepdims=True))
        alpha = jnp.exp(m_old - m_new)
        pexp = jnp.exp(sc - m_new)
        l_scr[...] = alpha * l_scr[...] + jnp.sum(pexp, axis=-1, keepdims=True)
        acc_scr[...] = alpha * acc_scr[...] + _dot(pexp, vs_ref[0, 0, pl.ds(k0, tk), :])
        m_scr[...] = m_new
        return carry

    lax.fori_loop(0, (qi * tq) // tk + 1, sel_step, 0)
    o_s = acc_scr[...] * (1.0 / l_scr[...])

    wk = min(WIN_KEYS, L)
    blk0 = jnp.maximum(qi - (wk // tq - 1), 0)
    w0 = pl.multiple_of(blk0 * tq, tq)
    kwin = kw_ref[0, 0, pl.ds(w0, wk), :]
    vwin = vw_ref[0, 0, pl.ds(w0, wk), :]
    sw = _dot_nt(q2, kwin)
    parts = []
    for u in range(wk // tq):
        delta = jnp.clip(qi - (blk0 + u), 0, TOEP_TILES - 1)
        parts.append(sw[:, u * tq:(u + 1) * tq] + toep_ref[0, delta])
    sw = jnp.concatenate(parts, axis=1)
    dist = qi * tq + _iota((tq, wk), 0) - (w0 + _iota((tq, wk), 1))
    in_win = (dist >= 0) & (dist < WINDOW)
    sw = jnp.where(in_win[None], sw.reshape(NSA_REP, tq, wk), NEG_INF).reshape(rows, wk)
    o_w = _dot(_softmax_rows(sw), vwin)

    gates = _sigmoid(misc_ref[0])
    lane = _iota((tq, LANES), 1)

    def gate_col(b, r):
        idx = SSD_HEADS + b * NSA_HEADS + g * NSA_REP + r
        return jnp.sum(jnp.where(lane == idx, gates, 0.0), axis=-1, keepdims=True)

    o_c3 = o_c.reshape(NSA_REP, tq, d)
    o_s3 = o_s.reshape(NSA_REP, tq, d)
    o_w3 = o_w.reshape(NSA_REP, tq, d)
    for r in range(NSA_REP):
        o_ref[0, 0, r] = gate_col(0, r) * o_c3[r] + gate_col(1, r) * o_s3[r] + gate_col(2, r) * o_w3[r]


def _prompt_bias_tables(rel_bias):
    tq = Q_BLOCK
    i = np.arange(tq)
    e = np.arange(CMP_NEAR)
    dist_c = i[:, None] + CMP_BLOCK * e[None, :] - (tq - 1)
    bkt_c = _rel_bucket_np(dist_c)
    rel = rel_bias.astype(F32)
    far = rel[REL_BUCKETS - 1]
    near = rel[bkt_c] - far
    near = near.transpose(2, 0, 1).reshape(NSA_GROUPS, NSA_REP * tq, CMP_NEAR)
    farc = jnp.broadcast_to(far.reshape(NSA_GROUPS, NSA_REP, 1, 1), (NSA_GROUPS, NSA_REP, tq, 1))
    farc = farc.reshape(NSA_GROUPS, NSA_REP * tq, 1)
    cols = list(_split3(near)) + list(_split3(farc))
    t4 = jnp.concatenate(cols, axis=-1)
    t4 = jnp.pad(t4, ((0, 0), (0, 0), (0, LANES - t4.shape[-1])))
    dl = np.arange(TOEP_TILES)
    dist_t = tq * dl[:, None, None] + i[None, :, None] - i[None, None, :]
    onehot = jax.nn.one_hot(_rel_bucket_np(dist_t).reshape(-1), REL_BUCKETS, dtype=F32)
    toep = jnp.einsum('bh,nb->hn', rel, onehot, precision=HI)
    toep = toep.reshape(NSA_GROUPS, NSA_REP, TOEP_TILES, tq, tq)
    toep = toep.transpose(0, 2, 1, 3, 4).reshape(NSA_GROUPS, TOEP_TILES, NSA_REP * tq, tq)
    return t4, toep


def nsa_prompt(p_rows, nseq, rel_bias):
    m = p_rows.shape[0]
    L = m // nseq
    tq = Q_BLOCK
    d = NSA_HEADDIM
    g = NSA_GROUPS
    prep_rows = min(PREP_ROWS, L)
    qr, kcm, vcm, ks, vs, kw, vw, kv_rows = nsa_prep(p_rows.reshape(m // prep_rows, prep_rows, IN0_PAD), nseq)
    t4, toep = _prompt_bias_tables(rel_bias)
    nq = L // tq
    ncb = L // CMP_BLOCK
    rows = NSA_REP * tq
    full = lambda n: pl.BlockSpec((1, 1, n, d), lambda s, gg, q: (s, gg, 0, 0))
    o = pl.pallas_call(
        _nsa_prompt_kernel,
        grid=(nseq, g, nq),
        in_specs=[pl.BlockSpec((1, 1, NSA_REP, tq, d), lambda s, gg, q: (s, gg, 0, q, 0)),
                  full(ncb), full(ncb), full(L), full(L), full(L), full(L),
                  pl.BlockSpec((1, tq, LANES), lambda s, gg, q: (s * nq + q, 0, COL_MISC // LANES)),
                  pl.BlockSpec((1, rows, LANES), lambda s, gg, q: (gg, 0, 0)),
                  pl.BlockSpec((1, TOEP_TILES, rows, tq), lambda s, gg, q: (gg, 0, 0, 0))],
        out_specs=pl.BlockSpec((1, 1, NSA_REP, tq, d), lambda s, gg, q: (s, gg, 0, q, 0)),
        out_shape=jax.ShapeDtypeStruct((nseq, g, NSA_REP, L, d), F32),
        scratch_shapes=[pltpu.VMEM((rows, 1), F32), pltpu.VMEM((rows, 1), F32), pltpu.VMEM((rows, d), F32)],
        compiler_params=_cparams(("parallel", "parallel", "arbitrary")),
        name="nsa_prompt",
    )(qr, kcm, vcm, ks, vs, kw, vw, p_rows.reshape(m // tq, tq, IN0_PAD), t4, toep)
    kv_rows = kv_rows.reshape(6, nseq, 1, L, g, d)
    return o.transpose(0, 3, 1, 2, 4).reshape(m, g * NSA_REP * d), [kv_rows[i] for i in range(6)]


SAMPLE_PAD = 128


def _nsa_sample_kernel(*refs, n_pages, t_new):
    pt_ref = refs[0]
    del pt_ref
    q_ref, gate_ref, kvs_ref, kvw_ref, wk_ref, wv_ref, bc_ref, bs_ref, bw_ref, ex_ref = refs[1:11]
    pages = refs[11:11 + 4 * n_pages]
    o_ref, nwk_ref, nwv_ref = refs[11 + 4 * n_pages:14 + 4 * n_pages]
    kc_scr, vc_scr, ks_scr, vs_scr, kw_scr, vw_scr = refs[14 + 4 * n_pages:]
    ck_pages = pages[0:n_pages]
    cv_pages = pages[n_pages:2 * n_pages]
    sk_pages = pages[2 * n_pages:3 * n_pages]
    sv_pages = pages[3 * n_pages:4 * n_pages]
    page = ck_pages[0].shape[2]
    past = n_pages * page
    wb = wk_ref.shape[2]
    nrow = q_ref.shape[1]
    per_page = page // CMP_BLOCK
    ncb = n_pages * per_page
    kd = NSA_GROUPS * NSA_HEADDIM

    def pad_tile(new):
        return jnp.concatenate([new, jnp.zeros((SAMPLE_PAD - t_new, kd), F32)], axis=0).astype(BF16)

    for pg in range(n_pages):
        ck = ck_pages[pg][0, 0]
        cv = cv_pages[pg][0, 0]
        kc_scr[pg * per_page:(pg + 1) * per_page, :] = jnp.mean(ck.reshape(per_page, CMP_BLOCK, kd), axis=1)
        vc_scr[pg * per_page:(pg + 1) * per_page, :] = jnp.mean(cv.reshape(per_page, CMP_BLOCK, kd), axis=1)
        ks_scr[pg * page:(pg + 1) * page, :] = sk_pages[pg][0, 0].astype(BF16)
        vs_scr[pg * page:(pg + 1) * page, :] = sv_pages[pg][0, 0].astype(BF16)
    kvs = kvs_ref[0]
    kvw = kvw_ref[0]
    ks_scr[past:past + SAMPLE_PAD, :] = pad_tile(kvs[:, :kd])
    vs_scr[past:past + SAMPLE_PAD, :] = pad_tile(kvs[:, kd:])
    wk = wk_ref[0, 0]
    wv = wv_ref[0, 0]
    kw_scr[0:wb, :] = wk.astype(BF16)
    vw_scr[0:wb, :] = wv.astype(BF16)
    kw_scr[wb:wb + SAMPLE_PAD, :] = pad_tile(kvw[:, :kd])
    vw_scr[wb:wb + SAMPLE_PAD, :] = pad_tile(kvw[:, kd:])
    nwk_ref[0, 0, 0:wb - t_new, :] = wk[t_new:, :]
    nwk_ref[0, 0, wb - t_new:wb, :] = kvw[:, :kd]
    nwv_ref[0, 0, 0:wb - t_new, :] = wv[t_new:, :]
    nwv_ref[0, 0, wb - t_new:wb, :] = kvw[:, kd:]

    q2 = q_ref[0]
    tok = _iota((nrow, 1), 0) % t_new

    p = _softmax_rows(_dot_nt(q2, kc_scr[...]) + bc_ref[...])
    o_c = _dot(p, vc_scr[...])
    ngt = NSA_GROUPS * t_new
    per_g = nrow // NSA_GROUPS
    ra = _iota((ngt, nrow), 0)
    rb = _iota((ngt, nrow), 1)
    same = (ra // t_new == rb // per_g) & (ra % t_new == rb % t_new)
    psum = _dot_hi(_ones_where(same, F32), p)
    ratio = SEL_BLOCK // CMP_BLOCK
    fold_t = _ones_where(_iota((LANES, ncb), 1) // ratio == _iota((LANES, ncb), 0), F32)
    imp_t = _dot_nt_hi(fold_t, psum)
    jj = _iota((LANES, ngt), 0)
    cur = (past + _iota((LANES, ngt), 1) % t_new) // SEL_BLOCK
    forced = (jj == 0) | (jj == cur) | (jj == cur - 1)
    score = jnp.where(forced, FORCE_SCORE, jnp.where(jj <= cur, imp_t, NEG_INF))
    sel_t = _topk_mask(score, SEL_TOPK, axis=0)
    ea = _iota((nrow, ngt), 0)
    eb = _iota((nrow, ngt), 1)
    spread = _ones_where((eb // t_new == ea // per_g) & (eb % t_new == ea % t_new), BF16)
    sel_rows = _dot_nt(spread, sel_t)
    chosen = jnp.dot(sel_rows.astype(BF16), ex_ref[...], preferred_element_type=F32)

    lk = past + SAMPLE_PAD
    kpos = _iota((nrow, lk), 1)
    keep = (chosen > 0.5) & (kpos <= past + tok)
    s = jnp.where(keep, _dot_nt(q2, ks_scr[...]) + bs_ref[...], NEG_INF)
    o_s = _dot(_softmax_rows(s), vs_scr[...])

    dist = wb + tok - _iota((nrow, wb + SAMPLE_PAD), 1)
    s = jnp.where((dist >= 0) & (dist < WINDOW), _dot_nt(q2, kw_scr[...]) + bw_ref[...], NEG_INF)
    o_w = _dot(_softmax_rows(s), vw_scr[...])

    gates = _sigmoid(gate_ref[0])
    o_ref[0] = gates[:, 0:1] * o_c + gates[:, 1:2] * o_s + gates[:, 2:3] * o_w


def nsa_sample(p_rows, nseq, rel_bias, caches, win_k, win_v, page_table):
    m = p_rows.shape[0]
    t = m // nseq
    d = NSA_HEADDIM
    g = NSA_GROUPS
    kd = g * d
    n_pages = page_table.shape[1]
    page = caches[0].shape[2]
    past = n_pages * page
    wb = win_k.shape[2]
    nrow = g * NSA_REP * t
    ncb = past // CMP_BLOCK
    lk = past + SAMPLE_PAD
    lw = wb + SAMPLE_PAD

    q = p_rows[:, COL_Q:COL_Q + NSA_HEADS * d].reshape(nseq, t, g, NSA_REP, d) * (d ** -0.5)
    q = q.transpose(0, 2, 3, 1, 4)
    eye_g = jnp.eye(g, dtype=F32)
    q2 = (q[:, :, :, :, None, :] * eye_g[None, :, None, None, :, None]).reshape(nseq, nrow, kd).astype(BF16)
    graw = p_rows[:, COL_MISC + SSD_HEADS:COL_MISC + SSD_HEADS + 3 * NSA_HEADS]
    graw = graw.reshape(nseq, t, 3, g, NSA_REP).transpose(0, 3, 4, 1, 2).reshape(nseq, nrow, 3)

    rel = rel_bias.astype(F32)
    tt = np.arange(t)
    head_rows = lambda tab: tab.transpose(2, 0, 1).reshape(g, NSA_REP, t, -1).reshape(nrow, -1)
    bias_c = head_rows(rel[_rel_bucket_np(past + tt[:, None] - (np.arange(ncb)[None, :] * CMP_BLOCK + CMP_BLOCK - 1))])
    bias_s = head_rows(rel[_rel_bucket_np(past + tt[:, None] - np.arange(lk)[None, :])])
    bias_w = head_rows(rel[_rel_bucket_np(wb + tt[:, None] - np.arange(lw)[None, :])])
    expand = jnp.asarray(np.arange(LANES)[:, None] == (np.arange(lk)[None, :] // SEL_BLOCK), BF16)

    p3 = p_rows.reshape(nseq, t, IN0_PAD)
    pt_flat = page_table.reshape(-1).astype(jnp.int32)
    c4 = [c.reshape(c.shape[0], c.shape[1], page, kd) for c in caches]
    wk4 = win_k.reshape(nseq, win_k.shape[1], wb, kd)
    wv4 = win_v.reshape(nseq, win_v.shape[1], wb, kd)
    const2 = lambda s, pt: (0, 0)
    in_specs = [pl.BlockSpec((1, nrow, kd), lambda s, pt: (s, 0, 0)),
                pl.BlockSpec((1, nrow, 3), lambda s, pt: (s, 0, 0)),
                pl.BlockSpec((1, t, 2 * kd), lambda s, pt: (s, 0, (COL_KV + 2 * kd) // (2 * kd))),
                pl.BlockSpec((1, t, 2 * kd), lambda s, pt: (s, 0, (COL_KV + 4 * kd) // (2 * kd))),
                pl.BlockSpec((1, 1, wb, kd), lambda s, pt: (s, 0, 0, 0)),
                pl.BlockSpec((1, 1, wb, kd), lambda s, pt: (s, 0, 0, 0)),
                pl.BlockSpec((nrow, ncb), const2), pl.BlockSpec((nrow, lk), const2),
                pl.BlockSpec((nrow, lw), const2), pl.BlockSpec((LANES, lk), const2)]
    args = [q2, graw, p3, p3, wk4, wv4, bias_c, bias_s, bias_w, expand]
    for c in c4:
        for pg in range(n_pages):
            in_specs.append(pl.BlockSpec((1, 1, page, kd),
                                         lambda s, pt, pg=pg: (pt[s * n_pages + pg], 0, 0, 0)))
            args.append(c)
    grid_spec = pltpu.PrefetchScalarGridSpec(
        num_scalar_prefetch=1,
        grid=(nseq,),
        in_specs=in_specs,
        out_specs=[pl.BlockSpec((1, nrow, kd), lambda s, pt: (s, 0, 0)),
                   pl.BlockSpec((1, 1, wb, kd), lambda s, pt: (s, 0, 0, 0)),
                   pl.BlockSpec((1, 1, wb, kd), lambda s, pt: (s, 0, 0, 0))],
        scratch_shapes=[pltpu.VMEM((ncb, kd), F32), pltpu.VMEM((ncb, kd), F32),
                        pltpu.VMEM((lk, kd), BF16), pltpu.VMEM((lk, kd), BF16),
                        pltpu.VMEM((lw, kd), BF16), pltpu.VMEM((lw, kd), BF16)])
    o, nwk, nwv = pl.pallas_call(
        functools.partial(_nsa_sample_kernel, n_pages=n_pages, t_new=t),
        grid_spec=grid_spec,
        out_shape=[jax.ShapeDtypeStruct((nseq, nrow, kd), F32),
                   jax.ShapeDtypeStruct((nseq, 1, wb, kd), F32),
                   jax.ShapeDtypeStruct((nseq, 1, wb, kd), F32)],
        compiler_params=_cparams(("arbitrary",)),
        name="nsa_sample",
    )(pt_flat, *args)
    o = o.reshape(nseq, g, NSA_REP, t, g, d)
    o = jnp.stack([o[:, gi, :, :, gi, :] for gi in range(g)], axis=1)
    o = o.transpose(0, 3, 1, 2, 4).reshape(m, g * NSA_REP * d)
    return o, nwk.reshape(nseq, 1, wb, g, d), nwv.reshape(nseq, 1, wb, g, d)


CONF_ROWS = 256


def _trunk(x, nseq, mem_k, mem_v, w, st):
    m, dmod = x.shape
    L = m // nseq
    p_rows = norm_matmul(x, w['norm_mix'][0], w['w_in0'])
    q_ssd = SSD_CHUNK if L % SSD_CHUNK == 0 else L
    p3 = p_rows.reshape(m // q_ssd, q_ssd, IN0_PAD)
    if st is None:
        y_ssd, h_pairs = ssd_mixer(p3, nseq, None, None, *w['ssd'])
        o_nsa, kv = nsa_prompt(p_rows, nseq, w['rel_bias'])
        wbuf = min(WINDOW, L)
    else:
        y_ssd, h_pairs = ssd_mixer(p3, nseq, st['ssd_conv'], _state_to_pairs(st['ssm'][:, 0]), *w['ssd'])
        o_nsa, new_wk, new_wv = nsa_sample(p_rows, nseq, w['rel_bias'], st['pages'], st['win_k'], st['win_v'],
                                           st['page_table'])
    ssm = _pairs_to_state(h_pairs)[:, None]
    pr = p_rows.reshape(nseq, L, IN0_PAD)
    dconv = SSD_HEADS * SSD_HEADDIM + 2 * SSD_GROUPS * SSD_STATE
    sconv = pr[:, L - (SSD_CONV - 1):, COL_X:COL_X + dconv][:, None]
    kd = NSA_GROUPS * NSA_HEADDIM
    if st is not None:
        kv = [pr[:, :, COL_KV + i * kd:COL_KV + (i + 1) * kd].reshape(nseq, 1, L, NSA_GROUPS, NSA_HEADDIM)
              for i in range(6)]
    if st is None:
        new_wk, new_wv = kv[4][:, :, L - wbuf:], kv[5][:, :, L - wbuf:]
    x = matmul_res([y_ssd.reshape(m, -1), o_nsa], [w['w_out0_ssd'], w['w_out0_nsa']], x)
    x = cross_attend(x, nseq, w['norm_cross'][0], w['mem_wq'][0], w['mem_wo'][0], mem_k, mem_v, 0)
    x = swiglu_ffn(x, w['norm_ffn'][0], w['ffn_wg'], w['ffn_wu'], w['ffn_wd'])
    ag = norm_matmul(x, w['norm_mix'][1], w['conf_w_in'])
    tc = CONF_ROWS if L % CONF_ROWS == 0 else L
    xo, conf_buf = conf_mixer(ag.reshape(m // tc, tc, -1), x.reshape(m // tc, tc, dmod), nseq,
                              None if st is None else st['conf_conv'], *w['conf'])
    x = xo.reshape(m, dmod)
    x = cross_attend(x, nseq, w['norm_cross'][1], w['mem_wq'][1], w['mem_wo'][1], mem_k, mem_v, 1)
    y = moe_ffn_final(x, w['norm_ffn'][1], w['moe_router'], w['moe_wg'], w['moe_wu'], w['moe_wd'], w['norm_final'])
    return (y.reshape(nseq, L, dmod), ssm, sconv, kv[0], kv[1], kv[2], kv[3], new_wk, new_wv, conf_buf[:, None])


def kernel(x_prompt, x_sample, mem_prompt, cache_mem_k, cache_mem_v, cache_nsa_cmp_k, cache_nsa_cmp_v,
           cache_nsa_sel_k, cache_nsa_sel_v, cache_nsa_win_k, cache_nsa_win_v, state_ssm, state_ssd_conv,
           state_conf_conv, page_table, norm_mix, norm_cross, norm_ffn, norm_final, w_in0, w_out0, ssd_conv_w,
           ssd_conv_b, ssd_dt_bias, ssd_a_log, ssd_d, ssd_norm, rel_bias, conf_w_in, conf_dw, conf_dw_b,
           conf_ln_g, conf_ln_b, conf_w_out, mem_wq, mem_wk, mem_wv, mem_wo, ffn_wg, ffn_wu, ffn_wd,
           moe_router, moe_wg, moe_wu, moe_wd):
    bf = lambda a: a.astype(BF16)
    d_ssd = SSD_HEADS * SSD_HEADDIM
    w = {
        'norm_mix': norm_mix, 'norm_cross': norm_cross, 'norm_ffn': norm_ffn, 'norm_final': norm_final,
        'w_in0': _reorder_in0(w_in0[0]),
        'w_out0_ssd': bf(w_out0[0, :d_ssd]), 'w_out0_nsa': bf(w_out0[0, d_ssd:]),
        'ssd': (ssd_conv_w[0], ssd_conv_b[0], ssd_dt_bias[0], ssd_a_log[0], ssd_d[0], ssd_norm[0]),
        'rel_bias': rel_bias,
        'conf_w_in': bf(conf_w_in[0]),
        'conf': (conf_dw[0], conf_dw_b[0], conf_ln_g[0], conf_ln_b[0], bf(conf_w_out[0])),
        'mem_wq': bf(mem_wq), 'mem_wo': bf(mem_wo),
        'ffn_wg': bf(ffn_wg[0]), 'ffn_wu': bf(ffn_wu[0]), 'ffn_wd': bf(ffn_wd[0]),
        'moe_router': moe_router[0], 'moe_wg': bf(moe_wg[0]), 'moe_wu': bf(moe_wu[0]), 'moe_wd': bf(moe_wd[0]),
    }
    bp, sp, dmod = x_prompt.shape
    bs, ss, _ = x_sample.shape
    mk_p, mv_p = mem_kv(mem_prompt, bf(mem_wk), bf(mem_wv))
    mem_shape = (bp, mem_wk.shape[0], mem_prompt.shape[1], MEM_HEADS, MEM_HEADDIM)
    outs_p = _trunk(x_prompt.reshape(bp * sp, dmod), bp, mk_p, mv_p, w, None)
    st = {'ssm': state_ssm, 'ssd_conv': state_ssd_conv, 'win_k': cache_nsa_win_k, 'win_v': cache_nsa_win_v,
          'conf_conv': state_conf_conv, 'page_table': page_table,
          'pages': (cache_nsa_cmp_k, cache_nsa_cmp_v, cache_nsa_sel_k, cache_nsa_sel_v)}
    outs_s = _trunk(x_sample.reshape(bs * ss, dmod), bs, cache_mem_k, cache_mem_v, w, st)
    return (outs_p[0], outs_s[0], mk_p.reshape(mem_shape), mv_p.reshape(mem_shape)) + outs_p[1:] + outs_s[1:]
```
